```python
import math
import jax, jax.numpy as jnp
from jax import lax
import numpy as np

D_MODEL = 1024
BATCH = 8
SEQ = 4096
DEPTH = 4

HEAD_DIM = 64
N_MIXERS = 3
SB_HEADS = D_MODEL // HEAD_DIM
FOX_HEADS = D_MODEL // HEAD_DIM
SWA_Q_HEADS = D_MODEL // HEAD_DIM
SWA_KV_HEADS = 4
WINDOW = 128
Q_BLOCK = 128
REL_BUCKETS = 32
REL_MAX_DIST = 128
D_FF = 4 * D_MODEL
PLE_DIM = 256
EPS = 1e-6

kernel_name = "interleaved_sb_fox_swa_hybrid"


def _n_layers_of(kind):
    return len(range(kind, DEPTH, N_MIXERS))


def rms_norm(x, g):
    xf = x.astype(jnp.float32)
    y = xf * lax.rsqrt(jnp.mean(xf * xf, axis=-1, keepdims=True) + EPS)
    return (y * g.astype(jnp.float32)).astype(x.dtype)


def _heads(t, n_heads):
    return t.reshape(t.shape[0], t.shape[1], n_heads, HEAD_DIM)


def stick_breaking_attention(q, k, v):
    B, S, H, Dh = q.shape
    nb = S // Q_BLOCK
    scale = Dh ** -0.5
    qb = q.reshape(B, nb, Q_BLOCK, H, Dh).transpose(1, 0, 2, 3, 4)
    key_pos = jnp.arange(S)

    def block(args):
        q_blk, start = args
        z = jnp.einsum('bqhd,bkhd->bhqk', q_blk, k,
                       preferred_element_type=jnp.float32) * scale
        q_pos = start + jnp.arange(Q_BLOCK)
        strict = key_pos[None, :] < q_pos[:, None]
        log_keep = jnp.where(strict, jax.nn.log_sigmoid(-z), 0.0)
        after = lax.cumsum(log_keep, axis=3, reverse=True) - log_keep
        w = jnp.where(strict, jnp.exp(jax.nn.log_sigmoid(z) + after), 0.0)
        return jnp.einsum('bhqk,bkhd->bqhd', w.astype(v.dtype), v)

    out = lax.map(block, (qb, jnp.arange(nb, dtype=jnp.int32) * Q_BLOCK))
    return out.transpose(1, 0, 2, 3, 4).reshape(B, S, H, Dh)


def forgetting_attention(q, k, v, log_f):
    B, S, H, Dh = q.shape
    nb = S // Q_BLOCK
    scale = Dh ** -0.5
    c = jnp.cumsum(log_f, axis=1).transpose(0, 2, 1)
    cb = c.reshape(B, H, nb, Q_BLOCK).transpose(2, 0, 1, 3)
    qb = q.reshape(B, nb, Q_BLOCK, H, Dh).transpose(1, 0, 2, 3, 4)
    key_pos = jnp.arange(S)

    def block(args):
        q_blk, c_blk, start = args
        s = jnp.einsum('bqhd,bkhd->bhqk', q_blk, k,
                       preferred_element_type=jnp.float32) * scale
        s = s + (c_blk[..., :, None] - c[:, :, None, :])
        q_pos = start + jnp.arange(Q_BLOCK)
        causal = key_pos[None, :] <= q_pos[:, None]
        w = jax.nn.softmax(jnp.where(causal, s, -jnp.inf), axis=-1)
        return jnp.einsum('bhqk,bkhd->bqhd', w.astype(v.dtype), v)

    out = lax.map(block, (qb, cb, jnp.arange(nb, dtype=jnp.int32) * Q_BLOCK))
    return out.transpose(1, 0, 2, 3, 4).reshape(B, S, H, Dh)


def t5_bucket(dist):
    max_exact = REL_BUCKETS // 2
    d = jnp.maximum(dist, 1).astype(jnp.float32)
    large = max_exact + (jnp.log(d / max_exact) / math.log(REL_MAX_DIST / max_exact)
                         * (REL_BUCKETS - max_exact)).astype(jnp.int32)
    large = jnp.minimum(large, REL_BUCKETS - 1)
    return jnp.where(dist < max_exact, dist, large)


def sliding_window_attention(q, k, v, sinks, rel_bias):
    B, S, Hq, Dh = q.shape
    Hkv = k.shape[2]
    G = Hq // Hkv
    nb = S // Q_BLOCK
    scale = Dh ** -0.5
    pad = ((0, 0), (Q_BLOCK, 0), (0, 0), (0, 0))
    kp = jnp.pad(k, pad).reshape(B, nb + 1, Q_BLOCK, Hkv, Dh)
    vp = jnp.pad(v, pad).reshape(B, nb + 1, Q_BLOCK, Hkv, Dh)
    kband = jnp.concatenate([kp[:, :-1], kp[:, 1:]], axis=2).transpose(1, 0, 2, 3, 4)
    vband = jnp.concatenate([vp[:, :-1], vp[:, 1:]], axis=2).transpose(1, 0, 2, 3, 4)
    qb = q.reshape(B, nb, Q_BLOCK, Hkv, G, Dh).transpose(1, 0, 2, 3, 4, 5)
    kpos = (jnp.arange(nb) * Q_BLOCK - Q_BLOCK)[:, None] + jnp.arange(2 * Q_BLOCK)[None, :]

    dist = jnp.arange(Q_BLOCK)[:, None] + Q_BLOCK - jnp.arange(2 * Q_BLOCK)[None, :]
    in_window = (dist >= 0) & (dist < WINDOW)
    bias = rel_bias.astype(jnp.float32)[t5_bucket(jnp.maximum(dist, 0))]
    bias = bias.transpose(2, 0, 1).reshape(Hkv, G, Q_BLOCK, 2 * Q_BLOCK)
    sink_logit = sinks.astype(jnp.float32).reshape(Hkv, G, 1, 1)

    def block(args):
        q_blk, k_blk, v_blk, kp_blk = args
        s = jnp.einsum('bqhgd,bkhd->bhgqk', q_blk, k_blk,
                       preferred_element_type=jnp.float32) * scale + bias
        valid = in_window & (kp_blk >= 0)[None, :]
        s = jnp.where(valid, s, -jnp.inf)
        sink = jnp.broadcast_to(sink_logit, s.shape[:-1] + (1,))
        w = jax.nn.softmax(jnp.concatenate([s, sink], axis=-1), axis=-1)[..., :-1]
        return jnp.einsum('bhgqk,bkhd->bqhgd', w.astype(v_blk.dtype), v_blk)

    out = lax.map(block, (qb, kband, vband, kpos))
    return out.transpose(1, 0, 2, 3, 4, 5).reshape(B, S, Hq, Dh)


def squared_relu_mlp(u, w_up, w_down):
    a = jax.nn.relu(u @ w_up)
    return (a * a) @ w_down


def setup_inputs(seed: int = 0) -> dict:
    key = jax.random.key(seed)
    ks = jax.random.split(key, 20)
    f32 = jnp.float32
    n_sb, n_fox, n_swa = _n_layers_of(0), _n_layers_of(1), _n_layers_of(2)
    d_in = D_MODEL ** -0.5
    attn_w = (SB_HEADS * HEAD_DIM) ** -0.5

    def nrm(k, shape, s):
        return jax.random.normal(k, shape, f32) * s

    fox_cols = 3 * FOX_HEADS * HEAD_DIM + FOX_HEADS
    swa_cols = (SWA_Q_HEADS + 2 * SWA_KV_HEADS) * HEAD_DIM
    return {
        "x": jax.random.normal(ks[0], (BATCH, SEQ, D_MODEL), f32),
        "p": jax.random.normal(ks[1], (DEPTH, BATCH, SEQ, PLE_DIM), f32),
        "attn_norm": 1.0 + nrm(ks[2], (DEPTH, D_MODEL), 0.02),
        "mlp_norm": 1.0 + nrm(ks[3], (DEPTH, D_MODEL), 0.02),
        "ple_norm": 1.0 + nrm(ks[4], (DEPTH, D_MODEL), 0.02),
        "final_norm": 1.0 + nrm(ks[5], (D_MODEL,), 0.02),
        "w_in_sb": nrm(ks[6], (n_sb, D_MODEL, 3 * SB_HEADS * HEAD_DIM), d_in),
        "w_out_sb": nrm(ks[7], (n_sb, SB_HEADS * HEAD_DIM, D_MODEL), attn_w),
        "w_in_fox": nrm(ks[8], (n_fox, D_MODEL, fox_cols), d_in),
        "b_forget": jax.random.uniform(ks[9], (n_fox, FOX_HEADS), f32, 1.0, 6.0),
        "w_out_fox": nrm(ks[10], (n_fox, FOX_HEADS * HEAD_DIM, D_MODEL), attn_w),
        "w_in_swa": nrm(ks[11], (n_swa, D_MODEL, swa_cols), d_in),
        "sinks": nrm(ks[12], (n_swa, SWA_Q_HEADS), 0.5),
        "w_out_swa": nrm(ks[13], (n_swa, SWA_Q_HEADS * HEAD_DIM, D_MODEL), attn_w),
        "rel_bias": nrm(ks[14], (REL_BUCKETS, SWA_Q_HEADS), 0.5),
        "w_up": nrm(ks[15], (DEPTH, D_MODEL, D_FF), d_in),
        "w_down": nrm(ks[16], (DEPTH, D_FF, D_MODEL), D_FF ** -0.5),
        "w_ple": nrm(ks[17], (DEPTH, PLE_DIM, D_MODEL), PLE_DIM ** -0.5),
        "w_ple_gate": nrm(ks[18], (DEPTH, D_MODEL, D_MODEL), d_in),
    }


def reference(x, p, attn_norm, mlp_norm, ple_norm, final_norm, w_in_sb, w_out_sb,
              w_in_fox, b_forget, w_out_fox, w_in_swa, sinks, w_out_swa, rel_bias,
              w_up, w_down, w_ple, w_ple_gate):
    qkv_w = SB_HEADS * HEAD_DIM
    h = x
    for i in range(DEPTH):
        kind = i % N_MIXERS
        j = i // N_MIXERS
        u = rms_norm(h, attn_norm[i])
        if kind == 0:
            proj = u @ w_in_sb[j]
            q = _heads(proj[..., :qkv_w], SB_HEADS)
            k = _heads(proj[..., qkv_w:2 * qkv_w], SB_HEADS)
            v = _heads(proj[..., 2 * qkv_w:], SB_HEADS)
            o = stick_breaking_attention(q, k, v)
            o = o.reshape(o.shape[0], o.shape[1], -1) @ w_out_sb[j]
        elif kind == 1:
            proj = u @ w_in_fox[j]
            fw = FOX_HEADS * HEAD_DIM
            q = _heads(proj[..., :fw], FOX_HEADS)
            k = _heads(proj[..., fw:2 * fw], FOX_HEADS)
            v = _heads(proj[..., 2 * fw:3 * fw], FOX_HEADS)
            f_logit = proj[..., 3 * fw:].astype(jnp.float32) + b_forget[j].astype(jnp.float32)
            o = forgetting_attention(q, k, v, jax.nn.log_sigmoid(f_logit))
            o = o.reshape(o.shape[0], o.shape[1], -1) @ w_out_fox[j]
        else:
            proj = u @ w_in_swa[j]
            qw = SWA_Q_HEADS * HEAD_DIM
            kw = SWA_KV_HEADS * HEAD_DIM
            q = _heads(proj[..., :qw], SWA_Q_HEADS)
            k = _heads(proj[..., qw:qw + kw], SWA_KV_HEADS)
            v = _heads(proj[..., qw + kw:], SWA_KV_HEADS)
            o = sliding_window_attention(q, k, v, sinks[j], rel_bias)
            o = o.reshape(o.shape[0], o.shape[1], -1) @ w_out_swa[j]
        h = h + o
        h = h + squared_relu_mlp(rms_norm(h, mlp_norm[i]), w_up[i], w_down[i])
        gate = jax.nn.sigmoid(rms_norm(h, ple_norm[i]) @ w_ple_gate[i])
        h = h + (p[i] @ w_ple[i]) * gate
    return rms_norm(h, final_norm)
```

```python
import functools
import math

import jax
import jax.numpy as jnp
from jax import lax
from jax.experimental import pallas as pl
from jax.experimental.pallas import tpu as pltpu

F32 = jnp.float32
BF16 = jnp.bfloat16

HEAD_DIM = 64
LANES = 128
N_MIXERS = 3
SWA_KV_HEADS = 4
WINDOW = 128
REL_BUCKETS = 32
REL_MAX_DIST = 128
EPS = 1e-6
SCALE = HEAD_DIM ** -0.5

VMEM_LIMIT = 48 * 1024 * 1024
TOKEN_TILE = 512
FF_TILE = 1024
SB_TQ, SB_TK = 256, 128
FOX_TQ, FOX_TK = 256, 128
SWA_TQ = 128
GATE_TILE = 512


def _params(*sem):
    return pltpu.CompilerParams(dimension_semantics=sem, vmem_limit_bytes=VMEM_LIMIT)


def _rms(x, g):
    return x * lax.rsqrt(jnp.mean(x * x, axis=-1, keepdims=True) + EPS) * g


def _neg_softplus(z):
    return -(jnp.maximum(z, 0.0) + jnp.log(1.0 + jnp.exp(-jnp.abs(z))))


def _norm_matmul_kernel(x_ref, g_ref, w_ref, o_ref, u_ref):
    @pl.when(pl.program_id(1) == 0)
    def _():
        u_ref[...] = _rms(x_ref[...], g_ref[...]).astype(BF16)

    o_ref[...] = jnp.dot(u_ref[...], w_ref[...], preferred_element_type=F32).astype(o_ref.dtype)


def norm_matmul(x, g, w, tn):
    t, d = x.shape
    n = w.shape[1]
    tm = TOKEN_TILE
    return pl.pallas_call(
        _norm_matmul_kernel,
        grid=(t // tm, n // tn),
        in_specs=[
            pl.BlockSpec((tm, d), lambda i, j: (i, 0)),
            pl.BlockSpec((1, d), lambda i, j: (0, 0)),
            pl.BlockSpec((d, tn), lambda i, j: (0, j)),
        ],
        out_specs=pl.BlockSpec((tm, tn), lambda i, j: (i, j)),
        out_shape=jax.ShapeDtypeStruct((t, n), BF16),
        scratch_shapes=[pltpu.VMEM((tm, d), BF16)],
        compiler_params=_params("parallel", "arbitrary"),
        name="norm_matmul",
    )(x, g.reshape(1, d), w)


def _proj_residual_kernel(o_ref, h_ref, w_ref, out_ref):
    out_ref[...] = h_ref[...] + jnp.dot(o_ref[...], w_ref[...], preferred_element_type=F32)


def proj_residual(o, h, w):
    t, d = h.shape
    tm = TOKEN_TILE
    return pl.pallas_call(
        _proj_residual_kernel,
        grid=(t // tm,),
        in_specs=[
            pl.BlockSpec((tm, d), lambda i: (i, 0)),
            pl.BlockSpec((tm, d), lambda i: (i, 0)),
            pl.BlockSpec((d, d), lambda i: (0, 0)),
        ],
        out_specs=pl.BlockSpec((tm, d), lambda i: (i, 0)),
        out_shape=jax.ShapeDtypeStruct((t, d), F32),
        compiler_params=_params("parallel"),
        name="proj_residual",
    )(o, h, w)


def _mlp_kernel(h_ref, g_ref, wu_ref, wd_ref, out_ref, u_ref):
    @pl.when(pl.program_id(1) == 0)
    def _():
        h = h_ref[...]
        u_ref[...] = _rms(h, g_ref[...]).astype(BF16)
        out_ref[...] = h

    a = jnp.maximum(jnp.dot(u_ref[...], wu_ref[...], preferred_element_type=F32), 0.0)
    out_ref[...] += jnp.dot((a * a).astype(BF16), wd_ref[...], preferred_element_type=F32)


def mlp_residual(h, g, w_up, w_down):
    t, d = h.shape
    f = w_up.shape[1]
    tm, tf = TOKEN_TILE, FF_TILE
    return pl.pallas_call(
        _mlp_kernel,
        grid=(t // tm, f // tf),
        in_specs=[
            pl.BlockSpec((tm, d), lambda i, j: (i, 0)),
            pl.BlockSpec((1, d), lambda i, j: (0, 0)),
            pl.BlockSpec((d, tf), lambda i, j: (0, j)),
            pl.BlockSpec((tf, d), lambda i, j: (j, 0)),
        ],
        out_specs=pl.BlockSpec((tm, d), lambda i, j: (i, 0)),
        out_shape=jax.ShapeDtypeStruct((t, d), F32),
        scratch_shapes=[pltpu.VMEM((tm, d), BF16)],
        compiler_params=_params("parallel", "arbitrary"),
        name="mlp_residual",
    )(h, g.reshape(1, d), w_up, w_down)


def _ple_kernel(h_ref, p_ref, g_ref, wg_ref, wp_ref, fn_ref, out_ref, *, final):
    h = h_ref[...]
    u = _rms(h, g_ref[...]).astype(BF16)
    gate_logit = jnp.dot(u, wg_ref[...], preferred_element_type=F32)
    gate = 1.0 / (1.0 + jnp.exp(-gate_logit))
    e = jnp.dot(p_ref[...].astype(BF16), wp_ref[...], preferred_element_type=F32)
    hn = h + e * gate
    if final:
        hn = _rms(hn, fn_ref[...])
    out_ref[...] = hn


def ple_residual(h, p, g, w_gate, w_ple, final_g, final):
    t, d = h.shape
    pd = p.shape[1]
    tm = TOKEN_TILE
    return pl.pallas_call(
        functools.partial(_ple_kernel, final=final),
        grid=(t // tm,),
        in_specs=[
            pl.BlockSpec((tm, d), lambda i: (i, 0)),
            pl.BlockSpec((tm, pd), lambda i: (i, 0)),
            pl.BlockSpec((1, d), lambda i: (0, 0)),
            pl.BlockSpec((d, d), lambda i: (0, 0)),
            pl.BlockSpec((pd, d), lambda i: (0, 0)),
            pl.BlockSpec((1, d), lambda i: (0, 0)),
        ],
        out_specs=pl.BlockSpec((tm, d), lambda i: (i, 0)),
        out_shape=jax.ShapeDtypeStruct((t, d), F32),
        compiler_params=_params("parallel"),
        name="ple_residual",
    )(h, p, g.reshape(1, d), w_gate, w_ple, final_g.reshape(1, d))


def _stack_heads(q):
    tq = q.shape[0]
    lane = lax.broadcasted_iota(jnp.int32, (tq, LANES), 1)
    zero = jnp.zeros_like(q)
    return jnp.concatenate(
        [jnp.where(lane < HEAD_DIM, q, zero), jnp.where(lane >= HEAD_DIM, q, zero)], axis=0)


def _unstack_heads(acc):
    tq = acc.shape[0] // 2
    lane = lax.broadcasted_iota(jnp.int32, (tq, LANES), 1)
    return jnp.where(lane < HEAD_DIM, acc[:tq], acc[tq:])


def _scores(qm, kb):
    return lax.dot_general(qm, kb, (((1,), (1,)), ((), ())), preferred_element_type=F32)


def _local_positions(tq, tk):
    row = lax.broadcasted_iota(jnp.int32, (2 * tq, tk), 0)
    col = lax.broadcasted_iota(jnp.int32, (2 * tq, tk), 1)
    return jnp.where(row >= tq, row - tq, row), col


def _sb_kernel(q_ref, k_ref, v_ref, u_ref, o_ref, *, tq, tk):
    qi = pl.program_id(2)
    nsub = tq // tk
    qm = _stack_heads(q_ref[0] * jnp.asarray(SCALE, BF16))
    qloc, col = _local_positions(tq, tk)
    umat = u_ref[...]

    def tile(j, carry, acc, diag_offset):
        start = pl.multiple_of(j * tk, tk)
        kb = k_ref[0, pl.ds(start, tk), :]
        vb = v_ref[0, pl.ds(start, tk), :]
        z = _scores(qm, kb)
        lk = _neg_softplus(z)
        if diag_offset is not None:
            strict = col + diag_offset * tk < qloc
            lk = jnp.where(strict, lk, 0.0)
        hi = lk.astype(BF16)
        lo = (lk - hi.astype(F32)).astype(BF16)
        ct = jnp.dot(jnp.concatenate([hi, lo], axis=1), umat, preferred_element_type=F32)
        w = jnp.exp(z + ct[:, :tk] + carry)
        if diag_offset is not None:
            w = jnp.where(strict, w, 0.0)
        acc = acc + jnp.dot(w.astype(BF16), vb, preferred_element_type=F32)
        return carry + ct[:, tk:], acc

    carry = jnp.zeros((2 * tq, tk), F32)
    acc = jnp.zeros((2 * tq, LANES), F32)
    for d in reversed(range(nsub)):
        carry, acc = tile(qi * nsub + d, carry, acc, d)

    def body(i, state):
        return tile(qi * nsub - 1 - i, state[0], state[1], None)

    carry, acc = lax.fori_loop(0, qi * nsub, body, (carry, acc))
    o_ref[0] = _unstack_heads(acc).astype(o_ref.dtype)


def _suffix_sum_matrix(tk):
    j = jnp.arange(2 * tk)[:, None] % tk
    s = jnp.arange(2 * tk)[None, :]
    return ((s >= tk) | (j >= s)).astype(BF16)


def sb_attention(qkv, n_pairs):
    b, s, _ = qkv.shape
    tq, tk = SB_TQ, SB_TK
    return pl.pallas_call(
        functools.partial(_sb_kernel, tq=tq, tk=tk),
        grid=(b, n_pairs, s // tq),
        in_specs=[
            pl.BlockSpec((1, tq, LANES), lambda bi, hp, qi: (bi, qi, hp)),
            pl.BlockSpec((1, s, LANES), lambda bi, hp, qi: (bi, 0, n_pairs + hp)),
            pl.BlockSpec((1, s, LANES), lambda bi, hp, qi: (bi, 0, 2 * n_pairs + hp)),
            pl.BlockSpec((2 * tk, 2 * tk), lambda bi, hp, qi: (0, 0)),
        ],
        out_specs=pl.BlockSpec((1, tq, LANES), lambda bi, hp, qi: (bi, qi, hp)),
        out_shape=jax.ShapeDtypeStruct((b, s, n_pairs * LANES), BF16),
        compiler_params=_params("parallel", "parallel", "arbitrary"),
        name="sb_attention",
    )(qkv, qkv, qkv, _suffix_sum_matrix(tk))


def _fox_gate_kernel(x_ref, g_ref, w_ref, b_ref, tri_ref, c_ref, carry_ref):
    @pl.when(pl.program_id(1) == 0)
    def _():
        carry_ref[...] = jnp.zeros_like(carry_ref)

    tm = x_ref.shape[1]
    u = _rms(x_ref[0], g_ref[...]).astype(BF16)
    logit = jnp.dot(u, w_ref[...], preferred_element_type=F32) + b_ref[...]
    log_f = _neg_softplus(-logit)
    hi = log_f.astype(BF16)
    r = log_f - hi.astype(F32)
    mid = r.astype(BF16)
    lo = (r - mid.astype(F32)).astype(BF16)
    tri = tri_ref[...]
    c = (jnp.dot(tri, hi, preferred_element_type=F32)
         + jnp.dot(tri, mid, preferred_element_type=F32)
         + jnp.dot(tri, lo, preferred_element_type=F32)) + carry_ref[0:1, :]
    c_ref[0] = c
    carry_ref[...] = jnp.broadcast_to(c[tm - 1:tm, :], carry_ref.shape)


def fox_cumulative_log_forget(h, g, w_f, b_f):
    b, s, d = h.shape
    nh = w_f.shape[1]
    tm = GATE_TILE
    w_pad = jnp.zeros((d, LANES), BF16).at[:, :nh].set(w_f)
    b_pad = jnp.zeros((1, LANES), F32).at[0, :nh].set(b_f)
    tri = (jnp.arange(tm)[:, None] >= jnp.arange(tm)[None, :]).astype(BF16)
    return pl.pallas_call(
        _fox_gate_kernel,
        grid=(b, s // tm),
        in_specs=[
            pl.BlockSpec((1, tm, d), lambda bi, si: (bi, si, 0)),
            pl.BlockSpec((1, d), lambda bi, si: (0, 0)),
            pl.BlockSpec((d, LANES), lambda bi, si: (0, 0)),
            pl.BlockSpec((1, LANES), lambda bi, si: (0, 0)),
            pl.BlockSpec((tm, tm), lambda bi, si: (0, 0)),
        ],
        out_specs=pl.BlockSpec((1, tm, LANES), lambda bi, si: (bi, si, 0)),
        out_shape=jax.ShapeDtypeStruct((b, s, LANES), F32),
        scratch_shapes=[pltpu.VMEM((8, LANES), F32)],
        compiler_params=_params("parallel", "arbitrary"),
        name="fox_gate",
    )(h, g.reshape(1, d), w_pad, b_pad, tri)


def _fox_kernel(q_ref, k_ref, v_ref, cq_ref, ck_ref, o_ref, *, tq, tk):
    hp = pl.program_id(1)
    qi = pl.program_id(2)
    nsub = tq // tk
    qm = _stack_heads(q_ref[0] * jnp.asarray(SCALE, BF16))
    qloc, col = _local_positions(tq, tk)

    cq = cq_ref[0]
    head_lane = lax.broadcasted_iota(jnp.int32, cq.shape, 1)
    c_rows = jnp.concatenate(
        [jnp.sum(jnp.where(head_lane == 2 * hp + a, cq, 0.0), axis=1, keepdims=True) for a in (0, 1)],
        axis=0)

    def tile(j, m, l, acc, diag_offset):
        start = pl.multiple_of(j * tk, tk)
        kb = k_ref[0, pl.ds(start, tk), :]
        vb = v_ref[0, pl.ds(start, tk), :]
        c_keys = jnp.concatenate(
            [jnp.broadcast_to(ck_ref[0, 0, a:a + 1, pl.ds(start, tk)], (tq, tk)) for a in (0, 1)],
            axis=0)
        s = _scores(qm, kb) + (c_rows - c_keys)
        if diag_offset is not None:
            s = jnp.where(col + diag_offset * tk <= qloc, s, -jnp.inf)
        m_new = jnp.maximum(m, jnp.max(s, axis=1, keepdims=True))
        alpha = jnp.exp(m - m_new)
        p = jnp.exp(s - m_new)
        l = alpha * l + jnp.sum(p, axis=1, keepdims=True)
        acc = alpha * acc + jnp.dot(p.astype(BF16), vb, preferred_element_type=F32)
        return m_new, l, acc

    m = jnp.full((2 * tq, 1), -jnp.inf, F32)
    l = jnp.zeros((2 * tq, 1), F32)
    acc = jnp.zeros((2 * tq, LANES), F32)
    for d in range(nsub):
        m, l, acc = tile(qi * nsub + d, m, l, acc, d)

    def body(i, state):
        return tile(i, state[0], state[1], state[2], None)

    m, l, acc = lax.fori_loop(0, qi * nsub, body, (m, l, acc))
    o_ref[0] = _unstack_heads(acc / l).astype(o_ref.dtype)


def fox_attention(qkv, c, n_pairs):
    b, s, _ = qkv.shape
    nh = 2 * n_pairs
    tq, tk = FOX_TQ, FOX_TK
    c_q = c[:, :, :nh]
    c_k = jnp.swapaxes(c_q, 1, 2).reshape(b, n_pairs, 2, s)
    return pl.pallas_call(
        functools.partial(_fox_kernel, tq=tq, tk=tk),
        grid=(b, n_pairs, s // tq),
        in_specs=[
            pl.BlockSpec((1, tq, LANES), lambda bi, hp, qi: (bi, qi, hp)),
            pl.BlockSpec((1, s, LANES), lambda bi, hp, qi: (bi, 0, n_pairs + hp)),
            pl.BlockSpec((1, s, LANES), lambda bi, hp, qi: (bi, 0, 2 * n_pairs + hp)),
            pl.BlockSpec((1, tq, nh), lambda bi, hp, qi: (bi, qi, 0)),
            pl.BlockSpec((1, 1, 2, s), lambda bi, hp, qi: (bi, hp, 0, 0)),
        ],
        out_specs=pl.BlockSpec((1, tq, LANES), lambda bi, hp, qi: (bi, qi, hp)),
        out_shape=jax.ShapeDtypeStruct((b, s, n_pairs * LANES), BF16),
        compiler_params=_params("parallel", "parallel", "arbitrary"),
        name="fox_attention",
    )(qkv, qkv, qkv, c_q, c_k)


def _swa_kernel(q_ref, kp_ref, kc_ref, vp_ref, vc_ref, bias_ref, sink_ref, o_ref, *, tq):
    qi = pl.program_id(2)
    qm = _stack_heads(q_ref[0] * jnp.asarray(SCALE, BF16))
    kcat = jnp.concatenate([kp_ref[0], kc_ref[0]], axis=0)
    vcat = jnp.concatenate([vp_ref[0], vc_ref[0]], axis=0)
    s = _scores(qm, kcat) + bias_ref[...].reshape(2 * tq, 2 * tq)
    qloc, col = _local_positions(tq, 2 * tq)
    dist = qloc + tq - col
    valid = (dist >= 0) & (dist < WINDOW) & ((col >= tq) | (qi > 0))
    s = jnp.where(valid, s, -jnp.inf)
    sink = sink_ref[0]
    m = jnp.maximum(jnp.max(s, axis=1, keepdims=True), sink)
    p = jnp.exp(s - m)
    denom = jnp.sum(p, axis=1, keepdims=True) + jnp.exp(sink - m)
    o = jnp.dot(p.astype(BF16), vcat, preferred_element_type=F32) / denom
    o_ref[0] = _unstack_heads(o).astype(o_ref.dtype)


def _t5_bucket(dist):
    max_exact = REL_BUCKETS // 2
    d = jnp.maximum(dist, 1).astype(F32)
    large = max_exact + (jnp.log(d / max_exact) / math.log(REL_MAX_DIST / max_exact)
                         * (REL_BUCKETS - max_exact)).astype(jnp.int32)
    large = jnp.minimum(large, REL_BUCKETS - 1)
    return jnp.where(dist < max_exact, dist, large)


def swa_attention(proj, sinks, rel_bias, n_pairs):
    b, s, _ = proj.shape
    tq = SWA_TQ
    nh = 2 * n_pairs
    group_pairs = n_pairs // SWA_KV_HEADS
    dist = jnp.arange(tq)[:, None] + tq - jnp.arange(2 * tq)[None, :]
    bias = rel_bias.astype(F32)[_t5_bucket(jnp.maximum(dist, 0))]
    bias = bias.transpose(2, 0, 1)
    sink_rows = jnp.broadcast_to(sinks.astype(F32).reshape(n_pairs, 2, 1, 1),
                                 (n_pairs, 2, tq, 1)).reshape(n_pairs, 2 * tq, 1)
    k_col = lambda hp: n_pairs + hp // group_pairs
    v_col = lambda hp: n_pairs + SWA_KV_HEADS + hp // group_pairs
    return pl.pallas_call(
        functools.partial(_swa_kernel, tq=tq),
        grid=(b, n_pairs, s // tq),
        in_specs=[
            pl.BlockSpec((1, tq, LANES), lambda bi, hp, qi: (bi, qi, hp)),
            pl.BlockSpec((1, tq, LANES), lambda bi, hp, qi: (bi, jnp.maximum(qi - 1, 0), k_col(hp))),
            pl.BlockSpec((1, tq, LANES), lambda bi, hp, qi: (bi, qi, k_col(hp))),
            pl.BlockSpec((1, tq, LANES), lambda bi, hp, qi: (bi, jnp.maximum(qi - 1, 0), v_col(hp))),
            pl.BlockSpec((1, tq, LANES), lambda bi, hp, qi: (bi, qi, v_col(hp))),
            pl.BlockSpec((2, tq, 2 * tq), lambda bi, hp, qi: (hp, 0, 0)),
            pl.BlockSpec((1, 2 * tq, 1), lambda bi, hp, qi: (hp, 0, 0)),
        ],
        out_specs=pl.BlockSpec((1, tq, LANES), lambda bi, hp, qi: (bi, qi, hp)),
        out_shape=jax.ShapeDtypeStruct((b, s, n_pairs * LANES), BF16),
        compiler_params=_params("parallel", "parallel", "arbitrary"),
        name="swa_attention",
    )(proj, proj, proj, proj, proj, bias, sink_rows)


def _duplicate_heads(w, n_heads):
    d = w.shape[0]
    w = w.reshape(d, n_heads, HEAD_DIM)
    return jnp.concatenate([w, w], axis=2).reshape(d, n_heads * LANES)


def kernel(x, p, attn_norm, mlp_norm, ple_norm, final_norm, w_in_sb, w_out_sb, w_in_fox, b_forget, w_out_fox,
           w_in_swa, sinks, w_out_swa, rel_bias, w_up, w_down, w_ple, w_ple_gate):
    b, s, d = x.shape
    depth = p.shape[0]
    t = b * s
    n_pairs = d // LANES
    h = x.reshape(t, d)
    for i in range(depth):
        kind = i % N_MIXERS
        j = i // N_MIXERS
        if kind == 0:
            qkv = norm_matmul(h, attn_norm[i], w_in_sb[j].astype(BF16), tn=1024)
            o = sb_attention(qkv.reshape(b, s, 3 * d), n_pairs)
            w_out = w_out_sb[j]
        elif kind == 1:
            w_in = w_in_fox[j].astype(BF16)
            qkv = norm_matmul(h, attn_norm[i], w_in[:, :3 * d], tn=1024)
            c = fox_cumulative_log_forget(h.reshape(b, s, d), attn_norm[i], w_in[:, 3 * d:], b_forget[j])
            o = fox_attention(qkv.reshape(b, s, 3 * d), c, n_pairs)
            w_out = w_out_fox[j]
        else:
            w_in = w_in_swa[j].astype(BF16)
            kw = SWA_KV_HEADS * HEAD_DIM
            w_in = jnp.concatenate(
                [w_in[:, :d], _duplicate_heads(w_in[:, d:d + kw], SWA_KV_HEADS),
                 _duplicate_heads(w_in[:, d + kw:], SWA_KV_HEADS)], axis=1)
            proj = norm_matmul(h, attn_norm[i], w_in, tn=1024)
            o = swa_attention(proj.reshape(b, s, 2 * d), sinks[j], rel_bias, n_pairs)
            w_out = w_out_swa[j]
        h = proj_residual(o.reshape(t, d), h, w_out.astype(BF16))
        h = mlp_residual(h, mlp_norm[i], w_up[i].astype(BF16), w_down[i].astype(BF16))
        h = ple_residual(h, p[i].reshape(t, p.shape[-1]), ple_norm[i], w_ple_gate[i].astype(BF16),
                         w_ple[i].astype(BF16), final_norm, final=(i == depth - 1))
    return h.reshape(b, s, d)
```

```python
import functools
import math

import jax
import jax.numpy as jnp
from jax import lax
from jax.experimental import pallas as pl
from jax.experimental.pallas import tpu as pltpu

F32 = jnp.float32
BF16 = jnp.bfloat16

HEAD_DIM = 64
LANES = 128
N_MIXERS = 3
SWA_KV_HEADS = 4
WINDOW = 128
REL_BUCKETS = 32
REL_MAX_DIST = 128
EPS = 1e-6
SCALE = HEAD_DIM ** -0.5

VMEM_LIMIT = 48 * 1024 * 1024
TOKEN_TILE = 512
FF_TILE = 1024
SB_TQ, SB_TK = 256, 128
FOX_TQ = 256
SWA_TQ = 128
GATE_TILE = 512


def _params(*sem):
    return pltpu.CompilerParams(dimension_semantics=sem, vmem_limit_bytes=VMEM_LIMIT)


def _rms(x, g):
    return x * lax.rsqrt(jnp.mean(x * x, axis=-1, keepdims=True) + EPS) * g


def _softplus(z):
    return jnp.maximum(z, 0.0) + jnp.log(1.0 + jnp.exp(-jnp.abs(z)))


def _norm_matmul_kernel(x_ref, g_ref, w_ref, o_ref, u_ref):
    @pl.when(pl.program_id(1) == 0)
    def _():
        u_ref[...] = _rms(x_ref[...], g_ref[...]).astype(BF16)

    o_ref[...] = jnp.dot(u_ref[...], w_ref[...], preferred_element_type=F32).astype(o_ref.dtype)


def norm_matmul(x, g, w, tn):
    t, d = x.shape
    n = w.shape[1]
    tm = TOKEN_TILE
    return pl.pallas_call(
        _norm_matmul_kernel,
        grid=(t // tm, n // tn),
        in_specs=[
            pl.BlockSpec((tm, d), lambda i, j: (i, 0)),
            pl.BlockSpec((1, d), lambda i, j: (0, 0)),
            pl.BlockSpec((d, tn), lambda i, j: (0, j)),
        ],
        out_specs=pl.BlockSpec((tm, tn), lambda i, j: (i, j)),
        out_shape=jax.ShapeDtypeStruct((t, n), BF16),
        scratch_shapes=[pltpu.VMEM((tm, d), BF16)],
        compiler_params=_params("parallel", "arbitrary"),
        name="norm_matmul",
    )(x, g.reshape(1, d), w)


def _proj_residual_kernel(o_ref, h_ref, w_ref, out_ref):
    out_ref[...] = h_ref[...] + jnp.dot(o_ref[...], w_ref[...], preferred_element_type=F32)


def proj_residual(o, h, w):
    t, d = h.shape
    tm = TOKEN_TILE
    return pl.pallas_call(
        _proj_residual_kernel,
        grid=(t // tm,),
        in_specs=[
            pl.BlockSpec((tm, d), lambda i: (i, 0)),
            pl.BlockSpec((tm, d), lambda i: (i, 0)),
            pl.BlockSpec((d, d), lambda i: (0, 0)),
        ],
        out_specs=pl.BlockSpec((tm, d), lambda i: (i, 0)),
        out_shape=jax.ShapeDtypeStruct((t, d), F32),
        compiler_params=_params("parallel"),
        name="proj_residual",
    )(o, h, w)


def _mlp_kernel(h_ref, g_ref, wu_ref, wd_ref, out_ref, u_ref):
    @pl.when(pl.program_id(1) == 0)
    def _():
        h = h_ref[...]
        u_ref[...] = _rms(h, g_ref[...]).astype(BF16)
        out_ref[...] = h

    a = jnp.maximum(jnp.dot(u_ref[...], wu_ref[...], preferred_element_type=F32), 0.0)
    out_ref[...] += jnp.dot((a * a).astype(BF16), wd_ref[...], preferred_element_type=F32)


def mlp_residual(h, g, w_up, w_down):
    t, d = h.shape
    f = w_up.shape[1]
    tm, tf = TOKEN_TILE, FF_TILE
    return pl.pallas_call(
        _mlp_kernel,
        grid=(t // tm, f // tf),
        in_specs=[
            pl.BlockSpec((tm, d), lambda i, j: (i, 0)),
            pl.BlockSpec((1, d), lambda i, j: (0, 0)),
            pl.BlockSpec((d, tf), lambda i, j: (0, j)),
            pl.BlockSpec((tf, d), lambda i, j: (j, 0)),
        ],
        out_specs=pl.BlockSpec((tm, d), lambda i, j: (i, 0)),
        out_shape=jax.ShapeDtypeStruct((t, d), F32),
        scratch_shapes=[pltpu.VMEM((tm, d), BF16)],
        compiler_params=_params("parallel", "arbitrary"),
        name="mlp_residual",
    )(h, g.reshape(1, d), w_up, w_down)


def _ple_kernel(h_ref, p_ref, g_ref, wg_ref, wp_ref, fn_ref, out_ref, *, final):
    h = h_ref[...]
    u = _rms(h, g_ref[...]).astype(BF16)
    gate_logit = jnp.dot(u, wg_ref[...], preferred_element_type=F32)
    gate = 1.0 / (1.0 + jnp.exp(-gate_logit))
    e = jnp.dot(p_ref[...].astype(BF16), wp_ref[...], preferred_element_type=F32)
    hn = h + e * gate
    if final:
        hn = _rms(hn, fn_ref[...])
    out_ref[...] = hn


def ple_residual(h, p, g, w_gate, w_ple, final_g, final):
    t, d = h.shape
    pd = p.shape[1]
    tm = TOKEN_TILE
    return pl.pallas_call(
        functools.partial(_ple_kernel, final=final),
        grid=(t // tm,),
        in_specs=[
            pl.BlockSpec((tm, d), lambda i: (i, 0)),
            pl.BlockSpec((tm, pd), lambda i: (i, 0)),
            pl.BlockSpec((1, d), lambda i: (0, 0)),
            pl.BlockSpec((d, d), lambda i: (0, 0)),
            pl.BlockSpec((pd, d), lambda i: (0, 0)),
            pl.BlockSpec((1, d), lambda i: (0, 0)),
        ],
        out_specs=pl.BlockSpec((tm, d), lambda i: (i, 0)),
        out_shape=jax.ShapeDtypeStruct((t, d), F32),
        compiler_params=_params("parallel"),
        name="ple_residual",
    )(h, p, g.reshape(1, d), w_gate, w_ple, final_g.reshape(1, d))


def _stack_heads(q):
    tq = q.shape[0]
    lane = lax.broadcasted_iota(jnp.int32, (tq, LANES), 1)
    zero = jnp.zeros_like(q)
    return jnp.concatenate(
        [jnp.where(lane < HEAD_DIM, q, zero), jnp.where(lane >= HEAD_DIM, q, zero)], axis=0)


def _unstack_heads(acc):
    tq = acc.shape[0] // 2
    lane = lax.broadcasted_iota(jnp.int32, (tq, LANES), 1)
    return jnp.where(lane < HEAD_DIM, acc[:tq], acc[tq:])


def _scores(qm, kb):
    return lax.dot_general(qm, kb, (((1,), (1,)), ((), ())), preferred_element_type=F32)


def _local_positions(tq, tk):
    row = lax.broadcasted_iota(jnp.int32, (2 * tq, tk), 0)
    col = lax.broadcasted_iota(jnp.int32, (2 * tq, tk), 1)
    return jnp.where(row >= tq, row - tq, row), col


def _sb_kernel(q_ref, k_ref, v_ref, u_ref, o_ref, *, tq, tk):
    qi = pl.program_id(2)
    nsub = tq // tk
    qm = _stack_heads(q_ref[0] * jnp.asarray(SCALE, BF16))
    qloc, col = _local_positions(tq, tk)
    umat = u_ref[...]

    def group(g, carry, acc, masked):
        start = pl.multiple_of(g * tq, tq)
        kb = k_ref[0, pl.ds(start, tq), :]
        vb = v_ref[0, pl.ds(start, tq), :]
        z_all = _scores(qm, kb)
        ws = [None] * nsub
        for d in reversed(range(nsub)):
            z = z_all[:, d * tk:(d + 1) * tk]
            sp = _softplus(z)
            if masked:
                strict = col + d * tk < qloc
                sp = jnp.where(strict, sp, 0.0)
            hi = sp.astype(BF16)
            lo = (sp - hi.astype(F32)).astype(BF16)
            ct = jnp.dot(jnp.concatenate([hi, lo], axis=1), umat, preferred_element_type=F32)
            w = jnp.exp(z + ct[:, :tk] + carry)
            if masked:
                w = jnp.where(strict, w, 0.0)
            ws[d] = w.astype(BF16)
            carry = carry + ct[:, tk:]
        acc = acc + jnp.dot(jnp.concatenate(ws, axis=1), vb, preferred_element_type=F32)
        return carry, acc

    carry = jnp.zeros((2 * tq, tk), F32)
    acc = jnp.zeros((2 * tq, LANES), F32)
    carry, acc = group(qi, carry, acc, True)

    def body(i, state):
        return group(qi - 1 - i, state[0], state[1], False)

    carry, acc = lax.fori_loop(0, qi, body, (carry, acc))
    o_ref[0] = _unstack_heads(acc).astype(o_ref.dtype)


def _suffix_sum_matrix(tk):
    j = jnp.arange(2 * tk)[:, None] % tk
    s = jnp.arange(2 * tk)[None, :]
    return -((s >= tk) | (j >= s)).astype(BF16)


def sb_attention(qkv, n_pairs):
    b, s, _ = qkv.shape
    tq, tk = SB_TQ, SB_TK
    return pl.pallas_call(
        functools.partial(_sb_kernel, tq=tq, tk=tk),
        grid=(b, n_pairs, s // tq),
        in_specs=[
            pl.BlockSpec((1, tq, LANES), lambda bi, hp, qi: (bi, qi, hp)),
            pl.BlockSpec((1, s, LANES), lambda bi, hp, qi: (bi, 0, n_pairs + hp)),
            pl.BlockSpec((1, s, LANES), lambda bi, hp, qi: (bi, 0, 2 * n_pairs + hp)),
            pl.BlockSpec((2 * tk, 2 * tk), lambda bi, hp, qi: (0, 0)),
        ],
        out_specs=pl.BlockSpec((1, tq, LANES), lambda bi, hp, qi: (bi, qi, hp)),
        out_shape=jax.ShapeDtypeStruct((b, s, n_pairs * LANES), BF16),
        compiler_params=_params("parallel", "parallel", "arbitrary"),
        name="sb_attention",
    )(qkv, qkv, qkv, _suffix_sum_matrix(tk))


def _fox_gate_kernel(x_ref, g_ref, w_ref, b_ref, tri_ref, c_ref, carry_ref):
    @pl.when(pl.program_id(1) == 0)
    def _():
        carry_ref[...] = jnp.zeros_like(carry_ref)

    tm = x_ref.shape[1]
    u = _rms(x_ref[0], g_ref[...]).astype(BF16)
    logit = jnp.dot(u, w_ref[...], preferred_element_type=F32) + b_ref[...]
    log_f = -_softplus(-logit)
    hi = log_f.astype(BF16)
    r = log_f - hi.astype(F32)
    mid = r.astype(BF16)
    lo = (r - mid.astype(F32)).astype(BF16)
    tri = tri_ref[...]
    c = (jnp.dot(tri, hi, preferred_element_type=F32)
         + jnp.dot(tri, mid, preferred_element_type=F32)
         + jnp.dot(tri, lo, preferred_element_type=F32)) + carry_ref[0:1, :]
    c_ref[0] = c
    carry_ref[...] = jnp.broadcast_to(c[tm - 1:tm, :], carry_ref.shape)


def fox_cumulative_log_forget(h, g, w_f, b_f):
    b, s, d = h.shape
    nh = w_f.shape[1]
    tm = GATE_TILE
    w_pad = jnp.zeros((d, LANES), BF16).at[:, :nh].set(w_f)
    b_pad = jnp.zeros((1, LANES), F32).at[0, :nh].set(b_f)
    tri = (jnp.arange(tm)[:, None] >= jnp.arange(tm)[None, :]).astype(BF16)
    return pl.pallas_call(
        _fox_gate_kernel,
        grid=(b, s // tm),
        in_specs=[
            pl.BlockSpec((1, tm, d), lambda bi, si: (bi, si, 0)),
            pl.BlockSpec((1, d), lambda bi, si: (0, 0)),
            pl.BlockSpec((d, LANES), lambda bi, si: (0, 0)),
            pl.BlockSpec((1, LANES), lambda bi, si: (0, 0)),
            pl.BlockSpec((tm, tm), lambda bi, si: (0, 0)),
        ],
        out_specs=pl.BlockSpec((1, tm, LANES), lambda bi, si: (bi, si, 0)),
        out_shape=jax.ShapeDtypeStruct((b, s, LANES), F32),
        scratch_shapes=[pltpu.VMEM((8, LANES), F32)],
        compiler_params=_params("parallel", "arbitrary"),
        name="fox_gate",
    )(h, g.reshape(1, d), w_pad, b_pad, tri)


def _fox_kernel(q_ref, k_ref, v_ref, cq_ref, ck_ref, o_ref, vext_ref, *, tq):
    hp = pl.program_id(1)
    qi = pl.program_id(2)

    @pl.when(qi == 0)
    def _():
        vext_ref[:, :LANES] = v_ref[0]
        vext_ref[:, LANES:] = jnp.ones((vext_ref.shape[0], LANES), BF16)

    qm = _stack_heads(q_ref[0] * jnp.asarray(SCALE, BF16))
    qloc, col = _local_positions(tq, tq)

    cq = cq_ref[0]
    head_lane = lax.broadcasted_iota(jnp.int32, cq.shape, 1)
    c_rows = jnp.concatenate(
        [jnp.sum(jnp.where(head_lane == 2 * hp + a, cq, 0.0), axis=1, keepdims=True) for a in (0, 1)],
        axis=0)

    def group(g, m, acc, masked):
        start = pl.multiple_of(g * tq, tq)
        kb = k_ref[0, pl.ds(start, tq), :]
        vb = vext_ref[pl.ds(start, tq), :]
        c_keys = jnp.concatenate(
            [jnp.broadcast_to(ck_ref[0, 0, a:a + 1, pl.ds(start, tq)], (tq, tq)) for a in (0, 1)],
            axis=0)
        s = _scores(qm, kb) + (c_rows - c_keys)
        if masked:
            s = jnp.where(col <= qloc, s, -jnp.inf)
        m_new = jnp.maximum(m, jnp.max(s, axis=1, keepdims=True))
        alpha = jnp.exp(m - m_new)
        p = jnp.exp(s - m_new)
        acc = alpha * acc + jnp.dot(p.astype(BF16), vb, preferred_element_type=F32)
        return m_new, acc

    m = jnp.full((2 * tq, 1), -jnp.inf, F32)
    acc = jnp.zeros((2 * tq, 2 * LANES), F32)
    m, acc = group(qi, m, acc, True)

    def body(i, state):
        return group(i, state[0], state[1], False)

    m, acc = lax.fori_loop(0, qi, body, (m, acc))
    o_ref[0] = _unstack_heads(acc[:, :LANES] / acc[:, LANES:]).astype(o_ref.dtype)


def fox_attention(qkv, c, n_pairs):
    b, s, _ = qkv.shape
    nh = 2 * n_pairs
    tq = FOX_TQ
    c_q = c[:, :, :nh]
    c_k = jnp.swapaxes(c_q, 1, 2).reshape(b, n_pairs, 2, s)
    return pl.pallas_call(
        functools.partial(_fox_kernel, tq=tq),
        grid=(b, n_pairs, s // tq),
        in_specs=[
            pl.BlockSpec((1, tq, LANES), lambda bi, hp, qi: (bi, qi, hp)),
            pl.BlockSpec((1, s, LANES), lambda bi, hp, qi: (bi, 0, n_pairs + hp)),
            pl.BlockSpec((1, s, LANES), lambda bi, hp, qi: (bi, 0, 2 * n_pairs + hp)),
            pl.BlockSpec((1, tq, nh), lambda bi, hp, qi: (bi, qi, 0)),
            pl.BlockSpec((1, 1, 2, s), lambda bi, hp, qi: (bi, hp, 0, 0)),
        ],
        out_specs=pl.BlockSpec((1, tq, LANES), lambda bi, hp, qi: (bi, qi, hp)),
        out_shape=jax.ShapeDtypeStruct((b, s, n_pairs * LANES), BF16),
        scratch_shapes=[pltpu.VMEM((s, 2 * LANES), BF16)],
        compiler_params=_params("parallel", "parallel", "arbitrary"),
        name="fox_attention",
    )(qkv, qkv, qkv, c_q, c_k)


def _swa_kernel(q_ref, kp_ref, kc_ref, vp_ref, vc_ref, bias_ref, sink_ref, o_ref, *, tq):
    qi = pl.program_id(2)
    qm = _stack_heads(q_ref[0] * jnp.asarray(SCALE, BF16))
    kcat = jnp.concatenate([kp_ref[0], kc_ref[0]], axis=0)
    vcat = jnp.concatenate([vp_ref[0], vc_ref[0]], axis=0)
    s = _scores(qm, kcat) + bias_ref[...].reshape(2 * tq, 2 * tq)
    qloc, col = _local_positions(tq, 2 * tq)
    dist = qloc + tq - col
    valid = (dist >= 0) & (dist < WINDOW) & ((col >= tq) | (qi > 0))
    s = jnp.where(valid, s, -jnp.inf)
    sink = sink_ref[0]
    m = jnp.maximum(jnp.max(s, axis=1, keepdims=True), sink)
    p = jnp.exp(s - m)
    denom = jnp.sum(p, axis=1, keepdims=True) + jnp.exp(sink - m)
    o = jnp.dot(p.astype(BF16), vcat, preferred_element_type=F32) / denom
    o_ref[0] = _unstack_heads(o).astype(o_ref.dtype)


def _t5_bucket(dist):
    max_exact = REL_BUCKETS // 2
    d = jnp.maximum(dist, 1).astype(F32)
    large = max_exact + (jnp.log(d / max_exact) / math.log(REL_MAX_DIST / max_exact)
                         * (REL_BUCKETS - max_exact)).astype(jnp.int32)
    large = jnp.minimum(large, REL_BUCKETS - 1)
    return jnp.where(dist < max_exact, dist, large)


def swa_attention(proj, sinks, rel_bias, n_pairs):
    b, s, _ = proj.shape
    tq = SWA_TQ
    nh = 2 * n_pairs
    group_pairs = n_pairs // SWA_KV_HEADS
    dist = jnp.arange(tq)[:, None] + tq - jnp.arange(2 * tq)[None, :]
    bias = rel_bias.astype(F32)[_t5_bucket(jnp.maximum(dist, 0))]
    bias = bias.transpose(2, 0, 1)
    sink_rows = jnp.broadcast_to(sinks.astype(F32).reshape(n_pairs, 2, 1, 1),
                                 (n_pairs, 2, tq, 1)).reshape(n_pairs, 2 * tq, 1)
    k_col = lambda hp: n_pairs + hp // group_pairs
    v_col = lambda hp: n_pairs + SWA_KV_HEADS + hp // group_pairs
    return pl.pallas_call(
        functools.partial(_swa_kernel, tq=tq),
        grid=(b, n_pairs, s // tq),
        in_specs=[
            pl.BlockSpec((1, tq, LANES), lambda bi, hp, qi: (bi, qi, hp)),
            pl.BlockSpec((1, tq, LANES), lambda bi, hp, qi: (bi, jnp.maximum(qi - 1, 0), k_col(hp))),
            pl.BlockSpec((1, tq, LANES), lambda bi, hp, qi: (bi, qi, k_col(hp))),
            pl.BlockSpec((1, tq, LANES), lambda bi, hp, qi: (bi, jnp.maximum(qi - 1, 0), v_col(hp))),
            pl.BlockSpec((1, tq, LANES), lambda bi, hp, qi: (bi, qi, v_col(hp))),
            pl.BlockSpec((2, tq, 2 * tq), lambda bi, hp, qi: (hp, 0, 0)),
            pl.BlockSpec((1, 2 * tq, 1), lambda bi, hp, qi: (hp, 0, 0)),
        ],
        out_specs=pl.BlockSpec((1, tq, LANES), lambda bi, hp, qi: (bi, qi, hp)),
        out_shape=jax.ShapeDtypeStruct((b, s, n_pairs * LANES), BF16),
        compiler_params=_params("parallel", "parallel", "arbitrary"),
        name="swa_attention",
    )(proj, proj, proj, proj, proj, bias, sink_rows)


def _duplicate_heads(w, n_heads):
    d = w.shape[0]
    w = w.reshape(d, n_heads, HEAD_DIM)
    return jnp.concatenate([w, w], axis=2).reshape(d, n_heads * LANES)


def kernel(x, p, attn_norm, mlp_norm, ple_norm, final_norm, w_in_sb, w_out_sb, w_in_fox, b_forget, w_out_fox,
           w_in_swa, sinks, w_out_swa, rel_bias, w_up, w_down, w_ple, w_ple_gate):
    b, s, d = x.shape
    depth = p.shape[0]
    t = b * s
    n_pairs = d // LANES
    h = x.reshape(t, d)
    for i in range(depth):
        kind = i % N_MIXERS
        j = i // N_MIXERS
        if kind == 0:
            qkv = norm_matmul(h, attn_norm[i], w_in_sb[j].astype(BF16), tn=1024)
            o = sb_attention(qkv.reshape(b, s, 3 * d), n_pairs)
            w_out = w_out_sb[j]
        elif kind == 1:
            w_in = w_in_fox[j].astype(BF16)
            qkv = norm_matmul(h, attn_norm[i], w_in[:, :3 * d], tn=1024)
            c = fox_cumulative_log_forget(h.reshape(b, s, d), attn_norm[i], w_in[:, 3 * d:], b_forget[j])
            o = fox_attention(qkv.reshape(b, s, 3 * d), c, n_pairs)
            w_out = w_out_fox[j]
        else:
            w_in = w_in_swa[j].astype(BF16)
            kw = SWA_KV_HEADS * HEAD_DIM
            w_in = jnp.concatenate(
                [w_in[:, :d], _duplicate_heads(w_in[:, d:d + kw], SWA_KV_HEADS),
                 _duplicate_heads(w_in[:, d + kw:], SWA_KV_HEADS)], axis=1)
            proj = norm_matmul(h, attn_norm[i], w_in, tn=1024)
            o = swa_attention(proj.reshape(b, s, 2 * d), sinks[j], rel_bias, n_pairs)
            w_out = w_out_swa[j]
        h = proj_residual(o.reshape(t, d), h, w_out.astype(BF16))
        h = mlp_residual(h, mlp_norm[i], w_up[i].astype(BF16), w_down[i].astype(BF16))
        h = ple_residual(h, p[i].reshape(t, p.shape[-1]), ple_norm[i], w_ple_gate[i].astype(BF16),
                         w_ple[i].astype(BF16), final_norm, final=(i == depth - 1))
    return h.reshape(b, s, d)
```

```python
import functools
import math

import jax
import jax.numpy as jnp
from jax import lax
from jax.experimental import pallas as pl
from jax.experimental.pallas import tpu as pltpu

F32 = jnp.float32
BF16 = jnp.bfloat16

HEAD_DIM = 64
LANES = 128
N_MIXERS = 3
SWA_KV_HEADS = 4
WINDOW = 128
REL_BUCKETS = 32
REL_MAX_DIST = 128
EPS = 1e-6
SCALE = HEAD_DIM ** -0.5
LOG2E = math.log2(math.e)
SIGN_BIT = -2 ** 31

VMEM_LIMIT = 48 * 1024 * 1024
TOKEN_TILE = 512
FF_TILE = 1024
SB_TQ = 256
ROW_CHUNK = 32
SB_STREAMS = 2
FOX_TQ = 256
FOX_STREAMS = 2
SWA_BLOCK = 128
SWA_BLOCKS_PER_STEP = 4
GATE_TILE = 512


def _params(*sem):
    return pltpu.CompilerParams(dimension_semantics=sem, vmem_limit_bytes=VMEM_LIMIT)


def _rms(x, g):
    return x * lax.rsqrt(jnp.mean(x * x, axis=-1, keepdims=True) + EPS) * g


def _softplus2(z2):
    neg_abs = lax.bitcast_convert_type(lax.bitcast_convert_type(z2, jnp.int32) | jnp.int32(SIGN_BIT), F32)
    return jnp.maximum(z2, 0.0) + LOG2E * jnp.log(1.0 + jnp.exp2(neg_abs))


def _softplus(z):
    return jnp.maximum(z, 0.0) + jnp.log(1.0 + jnp.exp(-jnp.abs(z)))


def _norm_matmul_kernel(x_ref, g_ref, w_ref, o_ref, u_ref):
    @pl.when(pl.program_id(1) == 0)
    def _():
        u_ref[...] = _rms(x_ref[...], g_ref[...]).astype(BF16)

    o_ref[...] = jnp.dot(u_ref[...], w_ref[...], preferred_element_type=F32).astype(o_ref.dtype)


def norm_matmul(x, g, w, tn):
    t, d = x.shape
    n = w.shape[1]
    tm = TOKEN_TILE
    return pl.pallas_call(
        _norm_matmul_kernel,
        grid=(t // tm, n // tn),
        in_specs=[
            pl.BlockSpec((tm, d), lambda i, j: (i, 0)),
            pl.BlockSpec((1, d), lambda i, j: (0, 0)),
            pl.BlockSpec((d, tn), lambda i, j: (0, j)),
        ],
        out_specs=pl.BlockSpec((tm, tn), lambda i, j: (i, j)),
        out_shape=jax.ShapeDtypeStruct((t, n), BF16),
        scratch_shapes=[pltpu.VMEM((tm, d), BF16)],
        compiler_params=_params("parallel", "arbitrary"),
        name="norm_matmul",
    )(x, g.reshape(1, d), w)


def _proj_residual_kernel(o_ref, h_ref, w_ref, out_ref):
    out_ref[...] = h_ref[...] + jnp.dot(o_ref[...], w_ref[...], preferred_element_type=F32)


def proj_residual(o, h, w):
    t, d = h.shape
    tm = TOKEN_TILE
    return pl.pallas_call(
        _proj_residual_kernel,
        grid=(t // tm,),
        in_specs=[
            pl.BlockSpec((tm, d), lambda i: (i, 0)),
            pl.BlockSpec((tm, d), lambda i: (i, 0)),
            pl.BlockSpec((d, d), lambda i: (0, 0)),
        ],
        out_specs=pl.BlockSpec((tm, d), lambda i: (i, 0)),
        out_shape=jax.ShapeDtypeStruct((t, d), F32),
        compiler_params=_params("parallel"),
        name="proj_residual",
    )(o, h, w)


def _mlp_kernel(h_ref, g_ref, wu_ref, wd_ref, out_ref, u_ref):
    @pl.when(pl.program_id(1) == 0)
    def _():
        h = h_ref[...]
        u_ref[...] = _rms(h, g_ref[...]).astype(BF16)
        out_ref[...] = h

    a = jnp.maximum(jnp.dot(u_ref[...], wu_ref[...], preferred_element_type=F32), 0.0)
    out_ref[...] += jnp.dot((a * a).astype(BF16), wd_ref[...], preferred_element_type=F32)


def mlp_residual(h, g, w_up, w_down):
    t, d = h.shape
    f = w_up.shape[1]
    tm, tf = TOKEN_TILE, FF_TILE
    return pl.pallas_call(
        _mlp_kernel,
        grid=(t // tm, f // tf),
        in_specs=[
            pl.BlockSpec((tm, d), lambda i, j: (i, 0)),
            pl.BlockSpec((1, d), lambda i, j: (0, 0)),
            pl.BlockSpec((d, tf), lambda i, j: (0, j)),
            pl.BlockSpec((tf, d), lambda i, j: (j, 0)),
        ],
        out_specs=pl.BlockSpec((tm, d), lambda i, j: (i, 0)),
        out_shape=jax.ShapeDtypeStruct((t, d), F32),
        scratch_shapes=[pltpu.VMEM((tm, d), BF16)],
        compiler_params=_params("parallel", "arbitrary"),
        name="mlp_residual",
    )(h, g.reshape(1, d), w_up, w_down)


def _ple_kernel(h_ref, p_ref, g_ref, wg_ref, wp_ref, fn_ref, out_ref, *, final):
    h = h_ref[...]
    u = _rms(h, g_ref[...]).astype(BF16)
    gate_logit = jnp.dot(u, wg_ref[...], preferred_element_type=F32)
    gate = 1.0 / (1.0 + jnp.exp(-gate_logit))
    e = jnp.dot(p_ref[...].astype(BF16), wp_ref[...], preferred_element_type=F32)
    hn = h + e * gate
    if final:
        hn = _rms(hn, fn_ref[...])
    out_ref[...] = hn


def ple_residual(h, p, g, w_gate, w_ple, final_g, final):
    t, d = h.shape
    pd = p.shape[1]
    tm = TOKEN_TILE
    return pl.pallas_call(
        functools.partial(_ple_kernel, final=final),
        grid=(t // tm,),
        in_specs=[
            pl.BlockSpec((tm, d), lambda i: (i, 0)),
            pl.BlockSpec((tm, pd), lambda i: (i, 0)),
            pl.BlockSpec((1, d), lambda i: (0, 0)),
            pl.BlockSpec((d, d), lambda i: (0, 0)),
            pl.BlockSpec((pd, d), lambda i: (0, 0)),
            pl.BlockSpec((1, d), lambda i: (0, 0)),
        ],
        out_specs=pl.BlockSpec((tm, d), lambda i: (i, 0)),
        out_shape=jax.ShapeDtypeStruct((t, d), F32),
        compiler_params=_params("parallel"),
        name="ple_residual",
    )(h, p, g.reshape(1, d), w_gate, w_ple, final_g.reshape(1, d))


def _stack_heads(q):
    tq = q.shape[0]
    lane = lax.broadcasted_iota(jnp.int32, (tq, LANES), 1)
    zero = jnp.zeros_like(q)
    return jnp.concatenate(
        [jnp.where(lane < HEAD_DIM, q, zero), jnp.where(lane >= HEAD_DIM, q, zero)], axis=0)


def _unstack_heads(acc):
    tq = acc.shape[0] // 2
    lane = lax.broadcasted_iota(jnp.int32, (tq, LANES), 1)
    return jnp.where(lane < HEAD_DIM, acc[:tq], acc[tq:])


def _scores(qm, kb):
    return lax.dot_general(qm, kb, (((1,), (1,)), ((), ())), preferred_element_type=F32)


def _sb_kernel(q_ref, k_ref, v_ref, u_ref, o_ref, z_ref, hi_ref, zms_ref, w_ref, carry_ref, acc_ref,
               *, tq, n_streams):
    qi = pl.program_id(2)
    n_groups = k_ref.shape[1] // tq
    lanes = [slice(p * LANES, (p + 1) * LANES) for p in range(n_streams)]
    qms = [_stack_heads(q_ref[0, :, lanes[p]]) for p in range(n_streams)]
    row_c = lax.broadcasted_iota(jnp.int32, (ROW_CHUNK, tq), 0)
    col_c = lax.broadcasted_iota(jnp.int32, (ROW_CHUNK, tq), 1)
    umat = u_ref[...]

    def scores(p, g):
        start = pl.multiple_of(g * tq, tq)
        return _scores(qms[p], k_ref[0, pl.ds(start, tq), lanes[p]])

    def trip(g, masked):
        v_start = pl.multiple_of(jnp.minimum(g + 1, n_groups - 1) * tq, tq)
        pvs = [jnp.dot(w_ref[p], v_ref[0, pl.ds(v_start, tq), lanes[p]], preferred_element_type=F32)
               for p in range(n_streams)]
        cts = []
        for p in range(n_streams):
            for r in range(0, 2 * tq, ROW_CHUNK):
                z = z_ref[p, r:r + ROW_CHUNK]
                sp = _softplus2(z)
                if masked:
                    sp = jnp.where(col_c < row_c + (r % tq), sp, 0.0)
                hi_ref[p, r:r + ROW_CHUNK] = sp.astype(BF16)
                zms_ref[p, r:r + ROW_CHUNK] = z - sp
            cts.append(jnp.dot(hi_ref[p], umat, preferred_element_type=F32))
            z_ref[p] = scores(p, jnp.maximum(g - 1, 0))
        for p in range(n_streams):
            acc_ref[p] += pvs[p]
        for p in range(n_streams):
            for r in range(0, 2 * tq, ROW_CHUNK):
                ct = cts[p][r:r + ROW_CHUNK]
                carry = carry_ref[p, r:r + ROW_CHUNK]
                e = zms_ref[p, r:r + ROW_CHUNK] + jnp.where(col_c == tq - 1, 0.0, ct)
                if masked:
                    e = jnp.where(col_c < row_c + (r % tq), e, -jnp.inf)
                w_ref[p, r:r + ROW_CHUNK] = jnp.concatenate(
                    [jnp.exp2(e[:, c * LANES:(c + 1) * LANES] + carry).astype(BF16) for c in range(tq // LANES)],
                    axis=1)
                carry_ref[p, r:r + ROW_CHUNK] = carry + jnp.broadcast_to(ct[:, tq - 1:tq], (ROW_CHUNK, LANES))

    carry_ref[...] = jnp.zeros_like(carry_ref)
    acc_ref[...] = jnp.zeros_like(acc_ref)
    w_ref[...] = jnp.zeros_like(w_ref)
    for p in range(n_streams):
        z_ref[p] = scores(p, qi)
    trip(qi, True)

    def body(i, _):
        trip(qi - 1 - i, False)
        return 0

    lax.fori_loop(0, qi, body, 0)
    for p in range(n_streams):
        acc = acc_ref[p] + jnp.dot(w_ref[p], v_ref[0, 0:tq, lanes[p]], preferred_element_type=F32)
        o_ref[0, :, lanes[p]] = _unstack_heads(acc).astype(o_ref.dtype)


def _later_keys_matrix(tq):
    j = jnp.arange(tq)[:, None]
    s = jnp.arange(tq)[None, :]
    return -((j > s) | (s == tq - 1)).astype(BF16)


def sb_attention(qkv, n_pairs):
    b, s, _ = qkv.shape
    tq, ns = SB_TQ, SB_STREAMS
    w = ns * LANES
    nblk = n_pairs // ns
    return pl.pallas_call(
        functools.partial(_sb_kernel, tq=tq, n_streams=ns),
        grid=(b, nblk, s // tq),
        in_specs=[
            pl.BlockSpec((1, tq, w), lambda bi, hb, qi: (bi, qi, hb)),
            pl.BlockSpec((1, s, w), lambda bi, hb, qi: (bi, 0, nblk + hb)),
            pl.BlockSpec((1, s, w), lambda bi, hb, qi: (bi, 0, 2 * nblk + hb)),
            pl.BlockSpec((tq, tq), lambda bi, hb, qi: (0, 0)),
        ],
        out_specs=pl.BlockSpec((1, tq, w), lambda bi, hb, qi: (bi, qi, hb)),
        out_shape=jax.ShapeDtypeStruct((b, s, n_pairs * LANES), BF16),
        scratch_shapes=[
            pltpu.VMEM((ns, 2 * tq, tq), F32),
            pltpu.VMEM((ns, 2 * tq, tq), BF16),
            pltpu.VMEM((ns, 2 * tq, tq), F32),
            pltpu.VMEM((ns, 2 * tq, tq), BF16),
            pltpu.VMEM((ns, 2 * tq, LANES), F32),
            pltpu.VMEM((ns, 2 * tq, LANES), F32),
        ],
        compiler_params=_params("parallel", "parallel", "arbitrary"),
        name="sb_attention",
    )(qkv, qkv, qkv, _later_keys_matrix(tq))


def _fox_gate_kernel(x_ref, g_ref, w_ref, b_ref, tri_ref, c_ref, carry_ref):
    @pl.when(pl.program_id(1) == 0)
    def _():
        carry_ref[...] = jnp.zeros_like(carry_ref)

    tm = x_ref.shape[1]
    u = _rms(x_ref[0], g_ref[...]).astype(BF16)
    logit = jnp.dot(u, w_ref[...], preferred_element_type=F32) + b_ref[...]
    log_f = -_softplus(-logit)
    hi = log_f.astype(BF16)
    r = log_f - hi.astype(F32)
    mid = r.astype(BF16)
    lo = (r - mid.astype(F32)).astype(BF16)
    tri = tri_ref[...]
    c = (jnp.dot(tri, hi, preferred_element_type=F32)
         + jnp.dot(tri, mid, preferred_element_type=F32)
         + jnp.dot(tri, lo, preferred_element_type=F32)) + carry_ref[0:1, :]
    c_ref[0] = c * LOG2E
    carry_ref[...] = jnp.broadcast_to(c[tm - 1:tm, :], carry_ref.shape)


def fox_cumulative_log_forget(h, g, w_f, b_f):
    b, s, d = h.shape
    nh = w_f.shape[1]
    tm = GATE_TILE
    w_pad = jnp.zeros((d, LANES), BF16).at[:, :nh].set(w_f)
    b_pad = jnp.zeros((1, LANES), F32).at[0, :nh].set(b_f)
    tri = (jnp.arange(tm)[:, None] >= jnp.arange(tm)[None, :]).astype(BF16)
    return pl.pallas_call(
        _fox_gate_kernel,
        grid=(b, s // tm),
        in_specs=[
            pl.BlockSpec((1, tm, d), lambda bi, si: (bi, si, 0)),
            pl.BlockSpec((1, d), lambda bi, si: (0, 0)),
            pl.BlockSpec((d, LANES), lambda bi, si: (0, 0)),
            pl.BlockSpec((1, LANES), lambda bi, si: (0, 0)),
            pl.BlockSpec((tm, tm), lambda bi, si: (0, 0)),
        ],
        out_specs=pl.BlockSpec((1, tm, LANES), lambda bi, si: (bi, si, 0)),
        out_shape=jax.ShapeDtypeStruct((b, s, LANES), F32),
        scratch_shapes=[pltpu.VMEM((8, LANES), F32)],
        compiler_params=_params("parallel", "arbitrary"),
        name="fox_gate",
    )(h, g.reshape(1, d), w_pad, b_pad, tri)


def _fox_kernel(q_ref, k_ref, v_ref, cq_ref, ck_ref, o_ref, vext_ref, s_ref, w_ref, m_ref, alpha_ref, acc_ref,
                *, tq, n_streams):
    hb = pl.program_id(1)
    qi = pl.program_id(2)
    n_groups = k_ref.shape[1] // tq
    lanes = [slice(p * LANES, (p + 1) * LANES) for p in range(n_streams)]
    ext = [slice(p * 2 * LANES, (p + 1) * 2 * LANES) for p in range(n_streams)]

    @pl.when(qi == 0)
    def _():
        for p in range(n_streams):
            vext_ref[:, p * 2 * LANES:p * 2 * LANES + LANES] = v_ref[0, :, lanes[p]]
            vext_ref[:, p * 2 * LANES + LANES:(p + 1) * 2 * LANES] = jnp.ones((vext_ref.shape[0], LANES), BF16)

    qms = [_stack_heads(q_ref[0, :, lanes[p]]) for p in range(n_streams)]
    row_c = lax.broadcasted_iota(jnp.int32, (ROW_CHUNK, tq), 0)
    col_c = lax.broadcasted_iota(jnp.int32, (ROW_CHUNK, tq), 1)

    cq = cq_ref[0]
    head_lane = lax.broadcasted_iota(jnp.int32, cq.shape, 1)
    c_rows = [jnp.concatenate(
        [jnp.broadcast_to(
            jnp.sum(jnp.where(head_lane == 2 * (hb * n_streams + p) + a, cq, 0.0), axis=1, keepdims=True),
            (tq, LANES)) for a in (0, 1)], axis=0) for p in range(n_streams)]

    def scores(p, g):
        start = pl.multiple_of(g * tq, tq)
        z = _scores(qms[p], k_ref[0, pl.ds(start, tq), lanes[p]])
        cols = []
        for c in range(tq // LANES):
            ck = jnp.concatenate(
                [jnp.broadcast_to(ck_ref[0, p, a:a + 1, pl.ds(start + c * LANES, LANES)], (tq, LANES))
                 for a in (0, 1)], axis=0)
            cols.append(z[:, c * LANES:(c + 1) * LANES] + (c_rows[p] - ck))
        return jnp.concatenate(cols, axis=1)

    def trip(g, masked):
        v_start = pl.multiple_of(jnp.minimum(g + 1, n_groups - 1) * tq, tq)
        pvs = [jnp.dot(w_ref[p], vext_ref[pl.ds(v_start, tq), ext[p]], preferred_element_type=F32)
               for p in range(n_streams)]
        for p in range(n_streams):
            for r in range(0, 2 * tq, ROW_CHUNK):
                rows = slice(r, r + ROW_CHUNK)
                acc_ref[p, rows] = acc_ref[p, rows] * jnp.concatenate([alpha_ref[p, rows]] * 2, axis=1) + pvs[p][rows]
                s = s_ref[p, rows]
                if masked:
                    s = jnp.where(col_c <= row_c + (r % tq), s, -jnp.inf)
                m_old = m_ref[p, rows]
                m_new = jnp.maximum(m_old, jnp.broadcast_to(jnp.max(s, axis=1, keepdims=True), (ROW_CHUNK, LANES)))
                alpha_ref[p, rows] = jnp.exp2(m_old - m_new)
                m_ref[p, rows] = m_new
                w_ref[p, rows] = jnp.concatenate(
                    [jnp.exp2(s[:, c * LANES:(c + 1) * LANES] - m_new).astype(BF16) for c in range(tq // LANES)],
                    axis=1)
            s_ref[p] = scores(p, jnp.maximum(g - 1, 0))

    m_ref[...] = jnp.full_like(m_ref, -jnp.inf)
    alpha_ref[...] = jnp.zeros_like(alpha_ref)
    acc_ref[...] = jnp.zeros_like(acc_ref)
    w_ref[...] = jnp.zeros_like(w_ref)
    for p in range(n_streams):
        s_ref[p] = scores(p, qi)
    trip(qi, True)

    def body(i, _):
        trip(qi - 1 - i, False)
        return 0

    lax.fori_loop(0, qi, body, 0)
    for p in range(n_streams):
        acc = (acc_ref[p] * jnp.concatenate([alpha_ref[p]] * 2, axis=1)
               + jnp.dot(w_ref[p], vext_ref[0:tq, ext[p]], preferred_element_type=F32))
        o_ref[0, :, lanes[p]] = _unstack_heads(acc[:, :LANES] / acc[:, LANES:]).astype(o_ref.dtype)


def fox_attention(qkv, c, n_pairs):
    b, s, _ = qkv.shape
    nh = 2 * n_pairs
    tq, ns = FOX_TQ, FOX_STREAMS
    w = ns * LANES
    nblk = n_pairs // ns
    c_q = c[:, :, :nh]
    c_k = jnp.swapaxes(c_q, 1, 2).reshape(b, n_pairs, 2, s)
    return pl.pallas_call(
        functools.partial(_fox_kernel, tq=tq, n_streams=ns),
        grid=(b, nblk, s // tq),
        in_specs=[
            pl.BlockSpec((1, tq, w), lambda bi, hb, qi: (bi, qi, hb)),
            pl.BlockSpec((1, s, w), lambda bi, hb, qi: (bi, 0, nblk + hb)),
            pl.BlockSpec((1, s, w), lambda bi, hb, qi: (bi, 0, 2 * nblk + hb)),
            pl.BlockSpec((1, tq, nh), lambda bi, hb, qi: (bi, qi, 0)),
            pl.BlockSpec((1, ns, 2, s), lambda bi, hb, qi: (bi, hb, 0, 0)),
        ],
        out_specs=pl.BlockSpec((1, tq, w), lambda bi, hb, qi: (bi, qi, hb)),
        out_shape=jax.ShapeDtypeStruct((b, s, n_pairs * LANES), BF16),
        scratch_shapes=[
            pltpu.VMEM((s, ns * 2 * LANES), BF16),
            pltpu.VMEM((ns, 2 * tq, tq), F32),
            pltpu.VMEM((ns, 2 * tq, tq), BF16),
            pltpu.VMEM((ns, 2 * tq, LANES), F32),
            pltpu.VMEM((ns, 2 * tq, LANES), F32),
            pltpu.VMEM((ns, 2 * tq, 2 * LANES), F32),
        ],
        compiler_params=_params("parallel", "parallel", "arbitrary"),
        name="fox_attention",
    )(qkv, qkv, qkv, c_q, c_k)


def _swa_kernel(q_ref, kp_ref, kc_ref, vp_ref, vc_ref, bias_ref, sink_ref, o_ref, *, blk, n_blocks):
    qi = pl.program_id(2)
    n_pairs_here = q_ref.shape[2] // LANES
    ones = jnp.ones((blk, LANES), BF16)
    k_all = jnp.concatenate([kp_ref[0], kc_ref[0]], axis=0)
    v_all = jnp.concatenate([vp_ref[0], vc_ref[0]], axis=0)
    col = lax.broadcasted_iota(jnp.int32, (2 * blk, 2 * blk), 1)
    for r in range(n_blocks):
        kcat = k_all[r * blk:(r + 2) * blk]
        vext = jnp.concatenate([v_all[r * blk:(r + 2) * blk], jnp.concatenate([ones, ones], axis=0)], axis=1)
        for a in range(n_pairs_here):
            qm = _stack_heads(q_ref[0, r * blk:(r + 1) * blk, a * LANES:(a + 1) * LANES])
            s = _scores(qm, kcat) + bias_ref[2 * a:2 * a + 2].reshape(2 * blk, 2 * blk)
            if r == 0:
                s = jnp.where((col >= blk) | (qi > 0), s, -jnp.inf)
            sink = sink_ref[a]
            m = jnp.maximum(jnp.max(s, axis=1, keepdims=True), sink)
            p = jnp.exp2(s - m)
            pv = jnp.dot(p.astype(BF16), vext, preferred_element_type=F32)
            o = pv[:, :LANES] / (pv[:, LANES:] + jnp.exp2(sink - m))
            o_ref[0, r * blk:(r + 1) * blk, a * LANES:(a + 1) * LANES] = _unstack_heads(o).astype(o_ref.dtype)


def _t5_bucket(dist):
    max_exact = REL_BUCKETS // 2
    d = jnp.maximum(dist, 1).astype(F32)
    large = max_exact + (jnp.log(d / max_exact) / math.log(REL_MAX_DIST / max_exact)
                         * (REL_BUCKETS - max_exact)).astype(jnp.int32)
    large = jnp.minimum(large, REL_BUCKETS - 1)
    return jnp.where(dist < max_exact, dist, large)


def swa_attention(proj, sinks, rel_bias, n_pairs):
    b, s, _ = proj.shape
    blk, nb = SWA_BLOCK, SWA_BLOCKS_PER_STEP
    tq = blk * nb
    group_pairs = n_pairs // SWA_KV_HEADS
    gw = group_pairs * LANES
    dist = jnp.arange(blk)[:, None] + blk - jnp.arange(2 * blk)[None, :]
    bias = rel_bias.astype(F32)[_t5_bucket(jnp.maximum(dist, 0))] * LOG2E
    bias = jnp.where(((dist >= 0) & (dist < WINDOW))[:, :, None], bias, -jnp.inf)
    bias = bias.transpose(2, 0, 1)
    sink_rows = jnp.broadcast_to((sinks.astype(F32) * LOG2E).reshape(n_pairs, 2, 1, 1),
                                 (n_pairs, 2, blk, 1)).reshape(n_pairs, 2 * blk, 1)
    return pl.pallas_call(
        functools.partial(_swa_kernel, blk=blk, n_blocks=nb),
        grid=(b, SWA_KV_HEADS, s // tq),
        in_specs=[
            pl.BlockSpec((1, tq, gw), lambda bi, gk, qi: (bi, qi, gk)),
            pl.BlockSpec((1, blk, LANES), lambda bi, gk, qi: (bi, jnp.maximum(qi * nb - 1, 0), n_pairs + gk)),
            pl.BlockSpec((1, tq, LANES), lambda bi, gk, qi: (bi, qi, n_pairs + gk)),
            pl.BlockSpec((1, blk, LANES),
                         lambda bi, gk, qi: (bi, jnp.maximum(qi * nb - 1, 0), n_pairs + SWA_KV_HEADS + gk)),
            pl.BlockSpec((1, tq, LANES), lambda bi, gk, qi: (bi, qi, n_pairs + SWA_KV_HEADS + gk)),
            pl.BlockSpec((2 * group_pairs, blk, 2 * blk), lambda bi, gk, qi: (gk, 0, 0)),
            pl.BlockSpec((group_pairs, 2 * blk, 1), lambda bi, gk, qi: (gk, 0, 0)),
        ],
        out_specs=pl.BlockSpec((1, tq, gw), lambda bi, gk, qi: (bi, qi, gk)),
        out_shape=jax.ShapeDtypeStruct((b, s, n_pairs * LANES), BF16),
        compiler_params=_params("parallel", "parallel", "arbitrary"),
        name="swa_attention",
    )(proj, proj, proj, proj, proj, bias, sink_rows)


def _duplicate_heads(w, n_heads):
    d = w.shape[0]
    w = w.reshape(d, n_heads, HEAD_DIM)
    return jnp.concatenate([w, w], axis=2).reshape(d, n_heads * LANES)


def _scale_q_columns(w_in, d):
    return jnp.concatenate([w_in[:, :d] * (SCALE * LOG2E), w_in[:, d:]], axis=1).astype(BF16)


def _swa_projection_weights(w_in, d):
    kw = SWA_KV_HEADS * HEAD_DIM
    w = _scale_q_columns(w_in, d)
    return jnp.concatenate(
        [w[:, :d], _duplicate_heads(w[:, d:d + kw], SWA_KV_HEADS), _duplicate_heads(w[:, d + kw:], SWA_KV_HEADS)],
        axis=1)


def kernel(x, p, attn_norm, mlp_norm, ple_norm, final_norm, w_in_sb, w_out_sb, w_in_fox, b_forget, w_out_fox,
           w_in_swa, sinks, w_out_swa, rel_bias, w_up, w_down, w_ple, w_ple_gate):
    b, s, d = x.shape
    depth = p.shape[0]
    t = b * s
    n_pairs = d // LANES
    h = x.reshape(t, d)
    for i in range(depth):
        kind = i % N_MIXERS
        j = i // N_MIXERS
        if kind == 0:
            qkv = norm_matmul(h, attn_norm[i], _scale_q_columns(w_in_sb[j], d), tn=1024)
            o = sb_attention(qkv.reshape(b, s, 3 * d), n_pairs)
            w_out = w_out_sb[j]
        elif kind == 1:
            w_in = w_in_fox[j]
            qkv = norm_matmul(h, attn_norm[i], _scale_q_columns(w_in[:, :3 * d], d), tn=1024)
            c = fox_cumulative_log_forget(h.reshape(b, s, d), attn_norm[i], w_in[:, 3 * d:].astype(BF16),
                                          b_forget[j])
            o = fox_attention(qkv.reshape(b, s, 3 * d), c, n_pairs)
            w_out = w_out_fox[j]
        else:
            proj = norm_matmul(h, attn_norm[i], _swa_projection_weights(w_in_swa[j], d), tn=1024)
            o = swa_attention(proj.reshape(b, s, 2 * d), sinks[j], rel_bias, n_pairs)
            w_out = w_out_swa[j]
        h = proj_residual(o.reshape(t, d), h, w_out.astype(BF16))
        h = mlp_residual(h, mlp_norm[i], w_up[i].astype(BF16), w_down[i].astype(BF16))
        h = ple_residual(h, p[i].reshape(t, p.shape[-1]), ple_norm[i], w_ple_gate[i].astype(BF16),
                         w_ple[i].astype(BF16), final_norm, final=(i == depth - 1))
    return h.reshape(b, s, d)
```

```python
import functools
import math

import jax
import jax.numpy as jnp
from jax import lax
from jax.experimental import pallas as pl
from jax.experimental.pallas import tpu as pltpu

F32 = jnp.float32
BF16 = jnp.bfloat16

HEAD_DIM = 64
LANES = 128
N_MIXERS = 3
SWA_KV_HEADS = 4
WINDOW = 128
REL_BUCKETS = 32
REL_MAX_DIST = 128
EPS = 1e-6
SCALE = HEAD_DIM ** -0.5
LOG2E = math.log2(math.e)
SIGN_BIT = -2 ** 31
UNDERFLOW_LOG2 = -152.0

VMEM_LIMIT = 48 * 1024 * 1024
TOKEN_TILE = 512
FF_TILE = 1024
SB_TQ = 256
ROW_CHUNK = 32
SB_STREAMS = 4
FOX_TQ = 256
FOX_STREAMS = 4
SWA_BLOCK = 128
SWA_BLOCKS_PER_STEP = 4
GATE_TILE = 512


def _params(*sem):
    return pltpu.CompilerParams(dimension_semantics=sem, vmem_limit_bytes=VMEM_LIMIT)


def _rms(x, g):
    return x * lax.rsqrt(jnp.mean(x * x, axis=-1, keepdims=True) + EPS) * g


def _softplus2(z2):
    neg_abs = lax.bitcast_convert_type(lax.bitcast_convert_type(z2, jnp.int32) | jnp.int32(SIGN_BIT), F32)
    return jnp.maximum(z2, 0.0) + LOG2E * jnp.log(1.0 + jnp.exp2(neg_abs))


def _softplus(z):
    return jnp.maximum(z, 0.0) + jnp.log(1.0 + jnp.exp(-jnp.abs(z)))


def _norm_matmul_kernel(x_ref, g_ref, w_ref, o_ref, u_ref):
    @pl.when(pl.program_id(1) == 0)
    def _():
        u_ref[...] = _rms(x_ref[...], g_ref[...]).astype(BF16)

    o_ref[...] = jnp.dot(u_ref[...], w_ref[...], preferred_element_type=F32).astype(o_ref.dtype)


def norm_matmul(x, g, w, tn):
    t, d = x.shape
    n = w.shape[1]
    tm = TOKEN_TILE
    return pl.pallas_call(
        _norm_matmul_kernel,
        grid=(t // tm, n // tn),
        in_specs=[
            pl.BlockSpec((tm, d), lambda i, j: (i, 0)),
            pl.BlockSpec((1, d), lambda i, j: (0, 0)),
            pl.BlockSpec((d, tn), lambda i, j: (0, j)),
        ],
        out_specs=pl.BlockSpec((tm, tn), lambda i, j: (i, j)),
        out_shape=jax.ShapeDtypeStruct((t, n), BF16),
        scratch_shapes=[pltpu.VMEM((tm, d), BF16)],
        compiler_params=_params("parallel", "arbitrary"),
        name="norm_matmul",
    )(x, g.reshape(1, d), w)


def _post_attention_kernel(o_ref, h_ref, p_ref, wo_ref, gm_ref, wu_ref, wd_ref, gp_ref, wg_ref, wp_ref, fn_ref,
                           out_ref, u_ref, *, final):
    f = pl.program_id(1)

    @pl.when(f == 0)
    def _():
        h = h_ref[...] + jnp.dot(o_ref[...], wo_ref[...], preferred_element_type=F32)
        u_ref[...] = _rms(h, gm_ref[...]).astype(BF16)
        out_ref[...] = h

    a = jnp.maximum(jnp.dot(u_ref[...], wu_ref[...], preferred_element_type=F32), 0.0)
    out_ref[...] += jnp.dot((a * a).astype(BF16), wd_ref[...], preferred_element_type=F32)

    @pl.when(f == pl.num_programs(1) - 1)
    def _():
        h = out_ref[...]
        u = _rms(h, gp_ref[...]).astype(BF16)
        gate = 1.0 / (1.0 + jnp.exp(-jnp.dot(u, wg_ref[...], preferred_element_type=F32)))
        e = jnp.dot(p_ref[...].astype(BF16), wp_ref[...], preferred_element_type=F32)
        hn = h + e * gate
        if final:
            hn = _rms(hn, fn_ref[...])
        out_ref[...] = hn


def post_attention(o, h, p, w_out, g_mlp, w_up, w_down, g_ple, w_gate, w_ple, final_g, final):
    t, d = h.shape
    f = w_up.shape[1]
    pd = p.shape[1]
    tm, tf = TOKEN_TILE, FF_TILE
    row = lambda i, j: (i, 0)
    fixed = lambda i, j: (0, 0)
    return pl.pallas_call(
        functools.partial(_post_attention_kernel, final=final),
        grid=(t // tm, f // tf),
        in_specs=[
            pl.BlockSpec((tm, d), row),
            pl.BlockSpec((tm, d), row),
            pl.BlockSpec((tm, pd), row),
            pl.BlockSpec((d, d), fixed),
            pl.BlockSpec((1, d), fixed),
            pl.BlockSpec((d, tf), lambda i, j: (0, j)),
            pl.BlockSpec((tf, d), lambda i, j: (j, 0)),
            pl.BlockSpec((1, d), fixed),
            pl.BlockSpec((d, d), fixed),
            pl.BlockSpec((pd, d), fixed),
            pl.BlockSpec((1, d), fixed),
        ],
        out_specs=pl.BlockSpec((tm, d), row),
        out_shape=jax.ShapeDtypeStruct((t, d), F32),
        scratch_shapes=[pltpu.VMEM((tm, d), BF16)],
        compiler_params=_params("parallel", "arbitrary"),
        name="post_attention",
    )(o, h, p, w_out, g_mlp.reshape(1, d), w_up, w_down, g_ple.reshape(1, d), w_gate, w_ple, final_g.reshape(1, d))


def _stack_heads(q):
    tq = q.shape[0]
    lane = lax.broadcasted_iota(jnp.int32, (tq, LANES), 1)
    zero = jnp.zeros_like(q)
    return jnp.concatenate(
        [jnp.where(lane < HEAD_DIM, q, zero), jnp.where(lane >= HEAD_DIM, q, zero)], axis=0)


def _unstack_heads(acc):
    tq = acc.shape[0] // 2
    lane = lax.broadcasted_iota(jnp.int32, (tq, LANES), 1)
    return jnp.where(lane < HEAD_DIM, acc[:tq], acc[tq:])


def _scores(qm, kb):
    return lax.dot_general(qm, kb, (((1,), (1,)), ((), ())), preferred_element_type=F32)


def _sb_kernel(q_ref, k_ref, v_ref, u_ref, o_ref, z_ref, hi_ref, zms_ref, w_ref, carry_ref, acc_ref,
               *, tq, n_streams):
    qi = pl.program_id(2)
    n_groups = k_ref.shape[1] // tq
    lanes = [slice(p * LANES, (p + 1) * LANES) for p in range(n_streams)]
    qms = [_stack_heads(q_ref[0, :, lanes[p]]) for p in range(n_streams)]
    row_c = lax.broadcasted_iota(jnp.int32, (ROW_CHUNK, tq), 0)
    col_c = lax.broadcasted_iota(jnp.int32, (ROW_CHUNK, tq), 1)
    lane_c = lax.broadcasted_iota(jnp.int32, (ROW_CHUNK, LANES), 1)
    umat = u_ref[...]

    def scores(p, g):
        start = pl.multiple_of(g * tq, tq)
        return _scores(qms[p], k_ref[0, pl.ds(start, tq), lanes[p]])

    def trip(g, masked):
        v_start = pl.multiple_of(jnp.minimum(g + 1, n_groups - 1) * tq, tq)
        pvs = [jnp.dot(w_ref[p], v_ref[0, pl.ds(v_start, tq), lanes[p]], preferred_element_type=F32)
               for p in range(n_streams)]
        cts = []
        for p in range(n_streams):
            for r in range(0, 2 * tq, ROW_CHUNK):
                z = z_ref[p, r:r + ROW_CHUNK]
                sp = _softplus2(z)
                if masked:
                    sp = jnp.where(col_c < row_c + (r % tq), sp, 0.0)
                hi_ref[p, r:r + ROW_CHUNK] = sp.astype(BF16)
                zms_ref[p, r:r + ROW_CHUNK] = z - sp
            cts.append(jnp.dot(hi_ref[p], umat, preferred_element_type=F32))
            z_ref[p] = scores(p, jnp.maximum(g - 1, 0))
        for p in range(n_streams):
            acc_ref[p] += pvs[p]
        for p in range(n_streams):
            for r in range(0, 2 * tq, ROW_CHUNK):
                ct = cts[p][r:r + ROW_CHUNK]
                carry = carry_ref[p, r:r + ROW_CHUNK]
                ct_last = ct[:, tq - LANES:]
                ct_keys = jnp.concatenate(
                    [ct[:, :tq - LANES], jnp.where(lane_c == LANES - 1, 0.0, ct_last)], axis=1)
                e = zms_ref[p, r:r + ROW_CHUNK] + ct_keys
                if masked:
                    e = jnp.where(col_c < row_c + (r % tq), e, -jnp.inf)
                w_ref[p, r:r + ROW_CHUNK] = jnp.concatenate(
                    [jnp.exp2(e[:, c * LANES:(c + 1) * LANES] + carry).astype(BF16) for c in range(tq // LANES)],
                    axis=1)
                carry_ref[p, r:r + ROW_CHUNK] = carry + jnp.broadcast_to(ct_last[:, LANES - 1:], (ROW_CHUNK, LANES))

    carry_ref[...] = jnp.zeros_like(carry_ref)
    acc_ref[...] = jnp.zeros_like(acc_ref)
    w_ref[...] = jnp.zeros_like(w_ref)
    for p in range(n_streams):
        z_ref[p] = scores(p, qi)
    trip(qi, True)

    def keep_going(state):
        g, top = state
        return (g >= 0) & (top > UNDERFLOW_LOG2)

    def body(state):
        g, _ = state
        trip(g, False)
        return g - 1, jnp.max(carry_ref[...])

    g_next, _ = lax.while_loop(keep_going, body, (qi - 1, jnp.max(carry_ref[...])))
    last = pl.multiple_of((g_next + 1) * tq, tq)
    for p in range(n_streams):
        acc = acc_ref[p] + jnp.dot(w_ref[p], v_ref[0, pl.ds(last, tq), lanes[p]], preferred_element_type=F32)
        o_ref[0, :, lanes[p]] = _unstack_heads(acc).astype(o_ref.dtype)


def _later_keys_matrix(tq):
    j = jnp.arange(tq)[:, None]
    s = jnp.arange(tq)[None, :]
    return -((j > s) | (s == tq - 1)).astype(BF16)


def sb_attention(qkv, n_pairs):
    b, s, _ = qkv.shape
    tq, ns = SB_TQ, SB_STREAMS
    w = ns * LANES
    nblk = n_pairs // ns
    return pl.pallas_call(
        functools.partial(_sb_kernel, tq=tq, n_streams=ns),
        grid=(b, nblk, s // tq),
        in_specs=[
            pl.BlockSpec((1, tq, w), lambda bi, hb, qi: (bi, qi, hb)),
            pl.BlockSpec((1, s, w), lambda bi, hb, qi: (bi, 0, nblk + hb)),
            pl.BlockSpec((1, s, w), lambda bi, hb, qi: (bi, 0, 2 * nblk + hb)),
            pl.BlockSpec((tq, tq), lambda bi, hb, qi: (0, 0)),
        ],
        out_specs=pl.BlockSpec((1, tq, w), lambda bi, hb, qi: (bi, qi, hb)),
        out_shape=jax.ShapeDtypeStruct((b, s, n_pairs * LANES), BF16),
        scratch_shapes=[
            pltpu.VMEM((ns, 2 * tq, tq), F32),
            pltpu.VMEM((ns, 2 * tq, tq), BF16),
            pltpu.VMEM((ns, 2 * tq, tq), F32),
            pltpu.VMEM((ns, 2 * tq, tq), BF16),
            pltpu.VMEM((ns, 2 * tq, LANES), F32),
            pltpu.VMEM((ns, 2 * tq, LANES), F32),
        ],
        compiler_params=_params("parallel", "parallel", "arbitrary"),
        name="sb_attention",
    )(qkv, qkv, qkv, _later_keys_matrix(tq))


def _fox_gate_kernel(x_ref, g_ref, w_ref, b_ref, tri_ref, c_ref, carry_ref):
    @pl.when(pl.program_id(1) == 0)
    def _():
        carry_ref[...] = jnp.zeros_like(carry_ref)

    tm = x_ref.shape[1]
    u = _rms(x_ref[0], g_ref[...]).astype(BF16)
    logit = jnp.dot(u, w_ref[...], preferred_element_type=F32) + b_ref[...]
    log_f = -_softplus(-logit)
    hi = log_f.astype(BF16)
    r = log_f - hi.astype(F32)
    mid = r.astype(BF16)
    lo = (r - mid.astype(F32)).astype(BF16)
    tri = tri_ref[...]
    c = (jnp.dot(tri, hi, preferred_element_type=F32)
         + jnp.dot(tri, mid, preferred_element_type=F32)
         + jnp.dot(tri, lo, preferred_element_type=F32)) + carry_ref[0:1, :]
    c_ref[0] = c * LOG2E
    carry_ref[...] = jnp.broadcast_to(c[tm - 1:tm, :], carry_ref.shape)


def fox_cumulative_log_forget(h, g, w_f, b_f):
    b, s, d = h.shape
    nh = w_f.shape[1]
    tm = GATE_TILE
    w_pad = jnp.zeros((d, LANES), BF16).at[:, :nh].set(w_f)
    b_pad = jnp.zeros((1, LANES), F32).at[0, :nh].set(b_f)
    tri = (jnp.arange(tm)[:, None] >= jnp.arange(tm)[None, :]).astype(BF16)
    return pl.pallas_call(
        _fox_gate_kernel,
        grid=(b, s // tm),
        in_specs=[
            pl.BlockSpec((1, tm, d), lambda bi, si: (bi, si, 0)),
            pl.BlockSpec((1, d), lambda bi, si: (0, 0)),
            pl.BlockSpec((d, LANES), lambda bi, si: (0, 0)),
            pl.BlockSpec((1, LANES), lambda bi, si: (0, 0)),
            pl.BlockSpec((tm, tm), lambda bi, si: (0, 0)),
        ],
        out_specs=pl.BlockSpec((1, tm, LANES), lambda bi, si: (bi, si, 0)),
        out_shape=jax.ShapeDtypeStruct((b, s, LANES), F32),
        scratch_shapes=[pltpu.VMEM((8, LANES), F32)],
        compiler_params=_params("parallel", "arbitrary"),
        name="fox_gate",
    )(h, g.reshape(1, d), w_pad, b_pad, tri)


def _fox_kernel(q_ref, k_ref, v_ref, cq_ref, ck_ref, o_ref, vext_ref, s_ref, w_ref, m_ref, alpha_ref, acc_ref,
                *, tq, n_streams):
    hb = pl.program_id(1)
    qi = pl.program_id(2)
    n_groups = k_ref.shape[1] // tq
    lanes = [slice(p * LANES, (p + 1) * LANES) for p in range(n_streams)]
    ext = [slice(p * 2 * LANES, (p + 1) * 2 * LANES) for p in range(n_streams)]

    @pl.when(qi == 0)
    def _():
        for p in range(n_streams):
            vext_ref[:, p * 2 * LANES:p * 2 * LANES + LANES] = v_ref[0, :, lanes[p]]
            vext_ref[:, p * 2 * LANES + LANES:(p + 1) * 2 * LANES] = jnp.ones((vext_ref.shape[0], LANES), BF16)

    qms = [_stack_heads(q_ref[0, :, lanes[p]]) for p in range(n_streams)]
    row_c = lax.broadcasted_iota(jnp.int32, (ROW_CHUNK, tq), 0)
    col_c = lax.broadcasted_iota(jnp.int32, (ROW_CHUNK, tq), 1)

    cq = cq_ref[0]
    head_lane = lax.broadcasted_iota(jnp.int32, cq.shape, 1)
    c_rows = [jnp.concatenate(
        [jnp.broadcast_to(
            jnp.sum(jnp.where(head_lane == 2 * (hb * n_streams + p) + a, cq, 0.0), axis=1, keepdims=True),
            (tq, LANES)) for a in (0, 1)], axis=0) for p in range(n_streams)]

    def scores(p, g):
        start = pl.multiple_of(g * tq, tq)
        z = _scores(qms[p], k_ref[0, pl.ds(start, tq), lanes[p]])
        cols = []
        for c in range(tq // LANES):
            ck = jnp.concatenate(
                [jnp.broadcast_to(ck_ref[0, p, a:a + 1, pl.ds(start + c * LANES, LANES)], (tq, LANES))
                 for a in (0, 1)], axis=0)
            cols.append(z[:, c * LANES:(c + 1) * LANES] + (c_rows[p] - ck))
        return jnp.concatenate(cols, axis=1)

    def trip(g, masked):
        v_start = pl.multiple_of(jnp.minimum(g + 1, n_groups - 1) * tq, tq)
        pvs = [jnp.dot(w_ref[p], vext_ref[pl.ds(v_start, tq), ext[p]], preferred_element_type=F32)
               for p in range(n_streams)]
        for p in range(n_streams):
            for r in range(0, 2 * tq, ROW_CHUNK):
                rows = slice(r, r + ROW_CHUNK)
                acc_ref[p, rows] = acc_ref[p, rows] * jnp.concatenate([alpha_ref[p, rows]] * 2, axis=1) + pvs[p][rows]
                s = s_ref[p, rows]
                if masked:
                    s = jnp.where(col_c <= row_c + (r % tq), s, -jnp.inf)
                m_old = m_ref[p, rows]
                m_new = jnp.maximum(m_old, jnp.broadcast_to(jnp.max(s, axis=1, keepdims=True), (ROW_CHUNK, LANES)))
                alpha_ref[p, rows] = jnp.exp2(m_old - m_new)
                m_ref[p, rows] = m_new
                w_ref[p, rows] = jnp.concatenate(
                    [jnp.exp2(s[:, c * LANES:(c + 1) * LANES] - m_new).astype(BF16) for c in range(tq // LANES)],
                    axis=1)
            s_ref[p] = scores(p, jnp.maximum(g - 1, 0))

    m_ref[...] = jnp.full_like(m_ref, -jnp.inf)
    alpha_ref[...] = jnp.zeros_like(alpha_ref)
    acc_ref[...] = jnp.zeros_like(acc_ref)
    w_ref[...] = jnp.zeros_like(w_ref)
    for p in range(n_streams):
        s_ref[p] = scores(p, qi)
    trip(qi, True)

    def body(i, _):
        trip(qi - 1 - i, False)
        return 0

    lax.fori_loop(0, qi, body, 0)
    for p in range(n_streams):
        acc = (acc_ref[p] * jnp.concatenate([alpha_ref[p]] * 2, axis=1)
               + jnp.dot(w_ref[p], vext_ref[0:tq, ext[p]], preferred_element_type=F32))
        o_ref[0, :, lanes[p]] = _unstack_heads(acc[:, :LANES] / acc[:, LANES:]).astype(o_ref.dtype)


def fox_attention(qkv, c, n_pairs):
    b, s, _ = qkv.shape
    nh = 2 * n_pairs
    tq, ns = FOX_TQ, FOX_STREAMS
    w = ns * LANES
    nblk = n_pairs // ns
    c_q = c[:, :, :nh]
    c_k = jnp.swapaxes(c_q, 1, 2).reshape(b, n_pairs, 2, s)
    return pl.pallas_call(
        functools.partial(_fox_kernel, tq=tq, n_streams=ns),
        grid=(b, nblk, s // tq),
        in_specs=[
            pl.BlockSpec((1, tq, w), lambda bi, hb, qi: (bi, qi, hb)),
            pl.BlockSpec((1, s, w), lambda bi, hb, qi: (bi, 0, nblk + hb)),
            pl.BlockSpec((1, s, w), lambda bi, hb, qi: (bi, 0, 2 * nblk + hb)),
            pl.BlockSpec((1, tq, nh), lambda bi, hb, qi: (bi, qi, 0)),
            pl.BlockSpec((1, ns, 2, s), lambda bi, hb, qi: (bi, hb, 0, 0)),
        ],
        out_specs=pl.BlockSpec((1, tq, w), lambda bi, hb, qi: (bi, qi, hb)),
        out_shape=jax.ShapeDtypeStruct((b, s, n_pairs * LANES), BF16),
        scratch_shapes=[
            pltpu.VMEM((s, ns * 2 * LANES), BF16),
            pltpu.VMEM((ns, 2 * tq, tq), F32),
            pltpu.VMEM((ns, 2 * tq, tq), BF16),
            pltpu.VMEM((ns, 2 * tq, LANES), F32),
            pltpu.VMEM((ns, 2 * tq, LANES), F32),
            pltpu.VMEM((ns, 2 * tq, 2 * LANES), F32),
        ],
        compiler_params=_params("parallel", "parallel", "arbitrary"),
        name="fox_attention",
    )(qkv, qkv, qkv, c_q, c_k)


def _swa_kernel(q_ref, kp_ref, kc_ref, vp_ref, vc_ref, bias_ref, sink_ref, o_ref, *, blk, n_blocks):
    qi = pl.program_id(2)
    n_pairs_here = q_ref.shape[2] // LANES
    ones = jnp.ones((blk, LANES), BF16)
    k_all = jnp.concatenate([kp_ref[0], kc_ref[0]], axis=0)
    v_all = jnp.concatenate([vp_ref[0], vc_ref[0]], axis=0)
    col = lax.broadcasted_iota(jnp.int32, (2 * blk, 2 * blk), 1)
    for r in range(n_blocks):
        kcat = k_all[r * blk:(r + 2) * blk]
        vext = jnp.concatenate([v_all[r * blk:(r + 2) * blk], jnp.concatenate([ones, ones], axis=0)], axis=1)
        for a in range(n_pairs_here):
            qm = _stack_heads(q_ref[0, r * blk:(r + 1) * blk, a * LANES:(a + 1) * LANES])
            s = _scores(qm, kcat) + bias_ref[2 * a:2 * a + 2].reshape(2 * blk, 2 * blk)
            if r == 0:
                s = jnp.where((col >= blk) | (qi > 0), s, -jnp.inf)
            sink = sink_ref[a]
            m = jnp.maximum(jnp.max(s, axis=1, keepdims=True), sink)
            p = jnp.exp2(s - m)
            pv = jnp.dot(p.astype(BF16), vext, preferred_element_type=F32)
            o = pv[:, :LANES] / (pv[:, LANES:] + jnp.exp2(sink - m))
            o_ref[0, r * blk:(r + 1) * blk, a * LANES:(a + 1) * LANES] = _unstack_heads(o).astype(o_ref.dtype)


def _t5_bucket(dist):
    max_exact = REL_BUCKETS // 2
    d = jnp.maximum(dist, 1).astype(F32)
    large = max_exact + (jnp.log(d / max_exact) / math.log(REL_MAX_DIST / max_exact)
                         * (REL_BUCKETS - max_exact)).astype(jnp.int32)
    large = jnp.minimum(large, REL_BUCKETS - 1)
    return jnp.where(dist < max_exact, dist, large)


def swa_attention(proj, sinks, rel_bias, n_pairs):
    b, s, _ = proj.shape
    blk, nb = SWA_BLOCK, SWA_BLOCKS_PER_STEP
    tq = blk * nb
    group_pairs = n_pairs // SWA_KV_HEADS
    gw = group_pairs * LANES
    dist = jnp.arange(blk)[:, None] + blk - jnp.arange(2 * blk)[None, :]
    bias = rel_bias.astype(F32)[_t5_bucket(jnp.maximum(dist, 0))] * LOG2E
    bias = jnp.where(((dist >= 0) & (dist < WINDOW))[:, :, None], bias, -jnp.inf)
    bias = bias.transpose(2, 0, 1)
    sink_rows = jnp.broadcast_to((sinks.astype(F32) * LOG2E).reshape(n_pairs, 2, 1, 1),
                                 (n_pairs, 2, blk, 1)).reshape(n_pairs, 2 * blk, 1)
    return pl.pallas_call(
        functools.partial(_swa_kernel, blk=blk, n_blocks=nb),
        grid=(b, SWA_KV_HEADS, s // tq),
        in_specs=[
            pl.BlockSpec((1, tq, gw), lambda bi, gk, qi: (bi, qi, gk)),
            pl.BlockSpec((1, blk, LANES), lambda bi, gk, qi: (bi, jnp.maximum(qi * nb - 1, 0), n_pairs + gk)),
            pl.BlockSpec((1, tq, LANES), lambda bi, gk, qi: (bi, qi, n_pairs + gk)),
            pl.BlockSpec((1, blk, LANES),
                         lambda bi, gk, qi: (bi, jnp.maximum(qi * nb - 1, 0), n_pairs + SWA_KV_HEADS + gk)),
            pl.BlockSpec((1, tq, LANES), lambda bi, gk, qi: (bi, qi, n_pairs + SWA_KV_HEADS + gk)),
            pl.BlockSpec((2 * group_pairs, blk, 2 * blk), lambda bi, gk, qi: (gk, 0, 0)),
            pl.BlockSpec((group_pairs, 2 * blk, 1), lambda bi, gk, qi: (gk, 0, 0)),
        ],
        out_specs=pl.BlockSpec((1, tq, gw), lambda bi, gk, qi: (bi, qi, gk)),
        out_shape=jax.ShapeDtypeStruct((b, s, n_pairs * LANES), BF16),
        compiler_params=_params("parallel", "parallel", "arbitrary"),
        name="swa_attention",
    )(proj, proj, proj, proj, proj, bias, sink_rows)


def _duplicate_heads(w, n_heads):
    d = w.shape[0]
    w = w.reshape(d, n_heads, HEAD_DIM)
    return jnp.concatenate([w, w], axis=2).reshape(d, n_heads * LANES)


def _scale_q_columns(w_in, d):
    return jnp.concatenate([w_in[:, :d] * (SCALE * LOG2E), w_in[:, d:]], axis=1).astype(BF16)


def _swa_projection_weights(w_in, d):
    kw = SWA_KV_HEADS * HEAD_DIM
    w = _scale_q_columns(w_in, d)
    return jnp.concatenate(
        [w[:, :d], _duplicate_heads(w[:, d:d + kw], SWA_KV_HEADS), _duplicate_heads(w[:, d + kw:], SWA_KV_HEADS)],
        axis=1)


def kernel(x, p, attn_norm, mlp_norm, ple_norm, final_norm, w_in_sb, w_out_sb, w_in_fox, b_forget, w_out_fox,
           w_in_swa, sinks, w_out_swa, rel_bias, w_up, w_down, w_ple, w_ple_gate):
    b, s, d = x.shape
    depth = p.shape[0]
    t = b * s
    n_pairs = d // LANES
    h = x.reshape(t, d)
    for i in range(depth):
        kind = i % N_MIXERS
        j = i // N_MIXERS
        if kind == 0:
            qkv = norm_matmul(h, attn_norm[i], _scale_q_columns(w_in_sb[j], d), tn=1024)
            o = sb_attention(qkv.reshape(b, s, 3 * d), n_pairs)
            w_out = w_out_sb[j]
        elif kind == 1:
            w_in = w_in_fox[j]
            qkv = norm_matmul(h, attn_norm[i], _scale_q_columns(w_in[:, :3 * d], d), tn=1024)
            c = fox_cumulative_log_forget(h.reshape(b, s, d), attn_norm[i], w_in[:, 3 * d:].astype(BF16),
                                          b_forget[j])
            o = fox_attention(qkv.reshape(b, s, 3 * d), c, n_pairs)
            w_out = w_out_fox[j]
        else:
            proj = norm_matmul(h, attn_norm[i], _swa_projection_weights(w_in_swa[j], d), tn=1024)
            o = swa_attention(proj.reshape(b, s, 2 * d), sinks[j], rel_bias, n_pairs)
            w_out = w_out_swa[j]
        h = post_attention(o.reshape(t, d), h, p[i].reshape(t, p.shape[-1]), w_out.astype(BF16), mlp_norm[i],
                           w_up[i].astype(BF16), w_down[i].astype(BF16), ple_norm[i], w_ple_gate[i].astype(BF16),
                           w_ple[i].astype(BF16), final_norm, final=(i == depth - 1))
    return h.reshape(b, s, d)
```

```python
import functools
import math

import jax
import jax.numpy as jnp
from jax import lax
from jax.experimental import pallas as pl
from jax.experimental.pallas import tpu as pltpu

F32 = jnp.float32
BF16 = jnp.bfloat16

HEAD_DIM = 64
LANES = 128
N_MIXERS = 3
SWA_KV_HEADS = 4
WINDOW = 128
REL_BUCKETS = 32
REL_MAX_DIST = 128
EPS = 1e-6
SCALE = HEAD_DIM ** -0.5
LOG2E = math.log2(math.e)
SIGN_BIT = -2 ** 31
UNDERFLOW_LOG2 = -152.0

VMEM_LIMIT = 48 * 1024 * 1024
TOKEN_TILE = 512
FF_TILE = 1024
SB_TQ = 256
ROW_CHUNK = 32
SB_STREAMS = 4
FOX_TQ = 256
FOX_STREAMS = 4
SWA_BLOCK = 128
SWA_BLOCKS_PER_STEP = 4
GATE_TILE = 512


def _params(*sem):
    return pltpu.CompilerParams(dimension_semantics=sem, vmem_limit_bytes=VMEM_LIMIT)


def _rms(x, g):
    return x * lax.rsqrt(jnp.mean(x * x, axis=-1, keepdims=True) + EPS) * g


def _softplus2(z2):
    neg_abs = lax.bitcast_convert_type(lax.bitcast_convert_type(z2, jnp.int32) | jnp.int32(SIGN_BIT), F32)
    return jnp.maximum(z2, 0.0) + LOG2E * jnp.log(1.0 + jnp.exp2(neg_abs))


def _softplus(z):
    return jnp.maximum(z, 0.0) + jnp.log(1.0 + jnp.exp(-jnp.abs(z)))


def _norm_matmul_kernel(x_ref, g_ref, w_ref, o_ref):
    u = _rms(x_ref[...], g_ref[...]).astype(BF16)
    o_ref[...] = jnp.dot(u, w_ref[...], preferred_element_type=F32).astype(o_ref.dtype)


def norm_matmul(x, g, w):
    t, d = x.shape
    n = w.shape[1]
    tm = TOKEN_TILE
    return pl.pallas_call(
        _norm_matmul_kernel,
        grid=(t // tm,),
        in_specs=[
            pl.BlockSpec((tm, d), lambda i: (i, 0)),
            pl.BlockSpec((1, d), lambda i: (0, 0)),
            pl.BlockSpec((d, n), lambda i: (0, 0)),
        ],
        out_specs=pl.BlockSpec((tm, n), lambda i: (i, 0)),
        out_shape=jax.ShapeDtypeStruct((t, n), BF16),
        compiler_params=_params("parallel"),
        name="norm_matmul",
    )(x, g.reshape(1, d), w)


def _post_attention_kernel(o_ref, h_ref, p_ref, wo_ref, gm_ref, wu_ref, wd_ref, gp_ref, wg_ref, wp_ref, fn_ref,
                           out_ref, u_ref, *, final):
    f = pl.program_id(1)

    @pl.when(f == 0)
    def _():
        h = h_ref[...] + jnp.dot(o_ref[...], wo_ref[...], preferred_element_type=F32)
        u_ref[...] = _rms(h, gm_ref[...]).astype(BF16)
        out_ref[...] = h

    a = jnp.maximum(jnp.dot(u_ref[...], wu_ref[...], preferred_element_type=F32), 0.0)
    out_ref[...] += jnp.dot((a * a).astype(BF16), wd_ref[...], preferred_element_type=F32)

    @pl.when(f == pl.num_programs(1) - 1)
    def _():
        h = out_ref[...]
        u = _rms(h, gp_ref[...]).astype(BF16)
        gate = 1.0 / (1.0 + jnp.exp(-jnp.dot(u, wg_ref[...], preferred_element_type=F32)))
        e = jnp.dot(p_ref[...].astype(BF16), wp_ref[...], preferred_element_type=F32)
        hn = h + e * gate
        if final:
            hn = _rms(hn, fn_ref[...])
        out_ref[...] = hn


def post_attention(o, h, p, w_out, g_mlp, w_up, w_down, g_ple, w_gate, w_ple, final_g, final):
    t, d = h.shape
    f = w_up.shape[1]
    pd = p.shape[1]
    tm, tf = TOKEN_TILE, FF_TILE
    row = lambda i, j: (i, 0)
    fixed = lambda i, j: (0, 0)
    return pl.pallas_call(
        functools.partial(_post_attention_kernel, final=final),
        grid=(t // tm, f // tf),
        in_specs=[
            pl.BlockSpec((tm, d), row),
            pl.BlockSpec((tm, d), row),
            pl.BlockSpec((tm, pd), row),
            pl.BlockSpec((d, d), fixed),
            pl.BlockSpec((1, d), fixed),
            pl.BlockSpec((d, tf), lambda i, j: (0, j)),
            pl.BlockSpec((tf, d), lambda i, j: (j, 0)),
            pl.BlockSpec((1, d), fixed),
            pl.BlockSpec((d, d), fixed),
            pl.BlockSpec((pd, d), fixed),
            pl.BlockSpec((1, d), fixed),
        ],
        out_specs=pl.BlockSpec((tm, d), row),
        out_shape=jax.ShapeDtypeStruct((t, d), F32),
        scratch_shapes=[pltpu.VMEM((tm, d), BF16)],
        compiler_params=_params("parallel", "arbitrary"),
        name="post_attention",
    )(o, h, p, w_out, g_mlp.reshape(1, d), w_up, w_down, g_ple.reshape(1, d), w_gate, w_ple, final_g.reshape(1, d))


def _stack_heads(q):
    tq = q.shape[0]
    lane = lax.broadcasted_iota(jnp.int32, (tq, LANES), 1)
    zero = jnp.zeros_like(q)
    return jnp.concatenate(
        [jnp.where(lane < HEAD_DIM, q, zero), jnp.where(lane >= HEAD_DIM, q, zero)], axis=0)


def _unstack_heads(acc):
    tq = acc.shape[0] // 2
    lane = lax.broadcasted_iota(jnp.int32, (tq, LANES), 1)
    return jnp.where(lane < HEAD_DIM, acc[:tq], acc[tq:])


def _scores(qm, kb):
    return lax.dot_general(qm, kb, (((1,), (1,)), ((), ())), preferred_element_type=F32)


def _sb_kernel(q_ref, k_ref, v_ref, u_ref, o_ref, z_ref, hi_ref, zms_ref, w_ref, carry_ref, acc_ref,
               *, tq, n_streams):
    qi = pl.program_id(2)
    lanes = [slice(p * LANES, (p + 1) * LANES) for p in range(n_streams)]
    qms = [_stack_heads(q_ref[0, :, lanes[p]]) for p in range(n_streams)]
    row_c = lax.broadcasted_iota(jnp.int32, (ROW_CHUNK, tq), 0)
    col_c = lax.broadcasted_iota(jnp.int32, (ROW_CHUNK, tq), 1)
    lane_c = lax.broadcasted_iota(jnp.int32, (ROW_CHUNK, LANES), 1)
    umat = u_ref[...]

    def scores(p, g):
        start = pl.multiple_of(g * tq, tq)
        return _scores(qms[p], k_ref[0, pl.ds(start, tq), lanes[p]])

    def trip(g, first):
        masked = first
        if not first:
            v_start = pl.multiple_of((g + 1) * tq, tq)
            pvs = [jnp.dot(w_ref[p], v_ref[0, pl.ds(v_start, tq), lanes[p]], preferred_element_type=F32)
                   for p in range(n_streams)]
        cts = []
        for p in range(n_streams):
            for r in range(0, 2 * tq, ROW_CHUNK):
                z = z_ref[p, r:r + ROW_CHUNK]
                sp = _softplus2(z)
                if masked:
                    sp = jnp.where(col_c < row_c + (r % tq), sp, 0.0)
                hi_ref[p, r:r + ROW_CHUNK] = sp.astype(BF16)
                zms_ref[p, r:r + ROW_CHUNK] = z - sp
            cts.append(jnp.dot(hi_ref[p], umat, preferred_element_type=F32))
            z_ref[p] = scores(p, jnp.maximum(g - 1, 0))
        for p in range(n_streams):
            if first:
                acc_ref[p] = jnp.zeros(acc_ref.shape[1:], F32)
            else:
                acc_ref[p] += pvs[p]
        for p in range(n_streams):
            for r in range(0, 2 * tq, ROW_CHUNK):
                ct = cts[p][r:r + ROW_CHUNK]
                ct_last = ct[:, tq - LANES:]
                ct_keys = jnp.concatenate(
                    [ct[:, :tq - LANES], jnp.where(lane_c == LANES - 1, 0.0, ct_last)], axis=1)
                e = zms_ref[p, r:r + ROW_CHUNK] + ct_keys
                if masked:
                    e = jnp.where(col_c < row_c + (r % tq), e, -jnp.inf)
                total = jnp.broadcast_to(ct_last[:, LANES - 1:], (ROW_CHUNK, LANES))
                if first:
                    cols = [e[:, c * LANES:(c + 1) * LANES] for c in range(tq // LANES)]
                    carry_ref[p, r:r + ROW_CHUNK] = total
                else:
                    carry = carry_ref[p, r:r + ROW_CHUNK]
                    cols = [e[:, c * LANES:(c + 1) * LANES] + carry for c in range(tq // LANES)]
                    carry_ref[p, r:r + ROW_CHUNK] = carry + total
                w_ref[p, r:r + ROW_CHUNK] = jnp.concatenate([jnp.exp2(c).astype(BF16) for c in cols], axis=1)

    for p in range(n_streams):
        z_ref[p] = scores(p, qi)
    trip(qi, True)

    def keep_going(state):
        g, top = state
        return (g >= 0) & (top > UNDERFLOW_LOG2)

    def body(state):
        g, _ = state
        trip(g, False)
        return g - 1, jnp.max(carry_ref[...])

    g_next, _ = lax.while_loop(keep_going, body, (qi - 1, jnp.max(carry_ref[...])))
    last = pl.multiple_of((g_next + 1) * tq, tq)
    for p in range(n_streams):
        acc = acc_ref[p] + jnp.dot(w_ref[p], v_ref[0, pl.ds(last, tq), lanes[p]], preferred_element_type=F32)
        o_ref[0, :, lanes[p]] = _unstack_heads(acc).astype(o_ref.dtype)


def _later_keys_matrix(tq):
    j = jnp.arange(tq)[:, None]
    s = jnp.arange(tq)[None, :]
    return -((j > s) | (s == tq - 1)).astype(BF16)


def sb_attention(qkv, n_pairs):
    b, s, _ = qkv.shape
    tq, ns = SB_TQ, SB_STREAMS
    w = ns * LANES
    nblk = n_pairs // ns
    return pl.pallas_call(
        functools.partial(_sb_kernel, tq=tq, n_streams=ns),
        grid=(b, nblk, s // tq),
        in_specs=[
            pl.BlockSpec((1, tq, w), lambda bi, hb, qi: (bi, qi, hb)),
            pl.BlockSpec((1, s, w), lambda bi, hb, qi: (bi, 0, nblk + hb)),
            pl.BlockSpec((1, s, w), lambda bi, hb, qi: (bi, 0, 2 * nblk + hb)),
            pl.BlockSpec((tq, tq), lambda bi, hb, qi: (0, 0)),
        ],
        out_specs=pl.BlockSpec((1, tq, w), lambda bi, hb, qi: (bi, qi, hb)),
        out_shape=jax.ShapeDtypeStruct((b, s, n_pairs * LANES), BF16),
        scratch_shapes=[
            pltpu.VMEM((ns, 2 * tq, tq), F32),
            pltpu.VMEM((ns, 2 * tq, tq), BF16),
            pltpu.VMEM((ns, 2 * tq, tq), F32),
            pltpu.VMEM((ns, 2 * tq, tq), BF16),
            pltpu.VMEM((ns, 2 * tq, LANES), F32),
            pltpu.VMEM((ns, 2 * tq, LANES), F32),
        ],
        compiler_params=_params("parallel", "parallel", "arbitrary"),
        name="sb_attention",
    )(qkv, qkv, qkv, _later_keys_matrix(tq))


def _fox_gate_kernel(x_ref, g_ref, w_ref, b_ref, tri_ref, c_ref, ct_ref, carry_ref):
    @pl.when(pl.program_id(1) == 0)
    def _():
        carry_ref[...] = jnp.zeros_like(carry_ref)

    tm = x_ref.shape[1]
    u = _rms(x_ref[0], g_ref[...]).astype(BF16)
    logit = jnp.dot(u, w_ref[...], preferred_element_type=F32) + b_ref[...]
    log_f = -_softplus(-logit)
    hi = log_f.astype(BF16)
    r = log_f - hi.astype(F32)
    mid = r.astype(BF16)
    lo = (r - mid.astype(F32)).astype(BF16)
    tri = tri_ref[...]
    c = (jnp.dot(tri, hi, preferred_element_type=F32)
         + jnp.dot(tri, mid, preferred_element_type=F32)
         + jnp.dot(tri, lo, preferred_element_type=F32)) + carry_ref[0:1, :]
    c2 = c * LOG2E
    c_ref[0] = c2
    ct_ref[0] = c2.T
    carry_ref[...] = jnp.broadcast_to(c[tm - 1:tm, :], carry_ref.shape)


def fox_cumulative_log_forget(h, g, w_f, b_f):
    b, s, d = h.shape
    nh = w_f.shape[1]
    tm = GATE_TILE
    w_pad = jnp.zeros((d, LANES), BF16).at[:, :nh].set(w_f)
    b_pad = jnp.zeros((1, LANES), F32).at[0, :nh].set(b_f)
    tri = (jnp.arange(tm)[:, None] >= jnp.arange(tm)[None, :]).astype(BF16)
    return pl.pallas_call(
        _fox_gate_kernel,
        grid=(b, s // tm),
        in_specs=[
            pl.BlockSpec((1, tm, d), lambda bi, si: (bi, si, 0)),
            pl.BlockSpec((1, d), lambda bi, si: (0, 0)),
            pl.BlockSpec((d, LANES), lambda bi, si: (0, 0)),
            pl.BlockSpec((1, LANES), lambda bi, si: (0, 0)),
            pl.BlockSpec((tm, tm), lambda bi, si: (0, 0)),
        ],
        out_specs=[pl.BlockSpec((1, tm, LANES), lambda bi, si: (bi, si, 0)),
                   pl.BlockSpec((1, LANES, tm), lambda bi, si: (bi, 0, si))],
        out_shape=[jax.ShapeDtypeStruct((b, s, LANES), F32), jax.ShapeDtypeStruct((b, LANES, s), F32)],
        scratch_shapes=[pltpu.VMEM((8, LANES), F32)],
        compiler_params=_params("parallel", "arbitrary"),
        name="fox_gate",
    )(h, g.reshape(1, d), w_pad, b_pad, tri)


def _fox_kernel(q_ref, k_ref, v_ref, cq_ref, ck_ref, o_ref, vext_ref, s_ref, w_ref, m_ref, alpha_ref, acc_ref,
                *, tq, n_streams):
    hb = pl.program_id(1)
    qi = pl.program_id(2)
    lanes = [slice(p * LANES, (p + 1) * LANES) for p in range(n_streams)]

    @pl.when(qi == 0)
    def _():
        lane = lax.broadcasted_iota(jnp.int32, (vext_ref.shape[0], LANES), 1)
        one = jnp.ones((vext_ref.shape[0], LANES), BF16)
        for p in range(n_streams):
            v = v_ref[0, :, lanes[p]]
            vext_ref[:, 2 * p * LANES:2 * (p + 1) * LANES] = jnp.concatenate(
                [jnp.where(lane < HEAD_DIM, v, one), jnp.where(lane < HEAD_DIM, one, v)], axis=1)

    qms = [_stack_heads(q_ref[0, :, lanes[p]]) for p in range(n_streams)]
    row_c = lax.broadcasted_iota(jnp.int32, (ROW_CHUNK, tq), 0)
    col_c = lax.broadcasted_iota(jnp.int32, (ROW_CHUNK, tq), 1)

    cq = cq_ref[0]
    head_lane = lax.broadcasted_iota(jnp.int32, cq.shape, 1)
    c_rows = [jnp.concatenate(
        [jnp.broadcast_to(
            jnp.sum(jnp.where(head_lane == 2 * (hb * n_streams + p) + a, cq, 0.0), axis=1, keepdims=True),
            (tq, LANES)) for a in (0, 1)], axis=0) for p in range(n_streams)]

    def scores(p, g):
        start = pl.multiple_of(g * tq, tq)
        z = _scores(qms[p], k_ref[0, pl.ds(start, tq), lanes[p]])
        cols = []
        for c in range(tq // LANES):
            ck = jnp.concatenate(
                [jnp.broadcast_to(ck_ref[0, 2 * p + a:2 * p + a + 1, pl.ds(start + c * LANES, LANES)], (tq, LANES))
                 for a in (0, 1)], axis=0)
            cols.append(z[:, c * LANES:(c + 1) * LANES] + (c_rows[p] - ck))
        return jnp.concatenate(cols, axis=1)

    def value_matmul(p, g):
        start = pl.multiple_of(g * tq, tq)
        halves = []
        for a in (0, 1):
            v_cols = slice((2 * p + a) * LANES, (2 * p + a + 1) * LANES)
            halves.append(jnp.dot(w_ref[p, a * tq:(a + 1) * tq], vext_ref[pl.ds(start, tq), v_cols],
                                  preferred_element_type=F32))
        return jnp.concatenate(halves, axis=0)

    def trip(g, first):
        if not first:
            pvs = [value_matmul(p, g + 1) for p in range(n_streams)]
        for p in range(n_streams):
            for r in range(0, 2 * tq, ROW_CHUNK):
                rows = slice(r, r + ROW_CHUNK)
                s = s_ref[p, rows]
                if first:
                    s = jnp.where(col_c <= row_c + (r % tq), s, -jnp.inf)
                    m_new = jnp.broadcast_to(jnp.max(s, axis=1, keepdims=True), (ROW_CHUNK, LANES))
                    acc_ref[p, rows] = jnp.zeros((ROW_CHUNK, LANES), F32)
                    alpha_ref[p, rows] = jnp.zeros((ROW_CHUNK, LANES), F32)
                else:
                    acc_ref[p, rows] = acc_ref[p, rows] * alpha_ref[p, rows] + pvs[p][rows]
                    m_old = m_ref[p, rows]
                    m_new = jnp.maximum(m_old, jnp.broadcast_to(jnp.max(s, axis=1, keepdims=True), (ROW_CHUNK, LANES)))
                    alpha_ref[p, rows] = jnp.exp2(m_old - m_new)
                m_ref[p, rows] = m_new
                w_ref[p, rows] = jnp.concatenate(
                    [jnp.exp2(s[:, c * LANES:(c + 1) * LANES] - m_new).astype(BF16) for c in range(tq // LANES)],
                    axis=1)
            s_ref[p] = scores(p, jnp.maximum(g - 1, 0))

    for p in range(n_streams):
        s_ref[p] = scores(p, qi)
    trip(qi, True)

    def body(i, _):
        trip(qi - 1 - i, False)
        return 0

    lax.fori_loop(0, qi, body, 0)
    for p in range(n_streams):
        acc = acc_ref[p] * alpha_ref[p] + value_matmul(p, 0)
        o_ref[0, :, lanes[p]] = _unstack_heads(acc / pltpu.roll(acc, HEAD_DIM, 1)).astype(o_ref.dtype)


def fox_attention(qkv, c, c_t, n_pairs):
    b, s, _ = qkv.shape
    tq, ns = FOX_TQ, FOX_STREAMS
    w = ns * LANES
    nblk = n_pairs // ns
    return pl.pallas_call(
        functools.partial(_fox_kernel, tq=tq, n_streams=ns),
        grid=(b, nblk, s // tq),
        in_specs=[
            pl.BlockSpec((1, tq, w), lambda bi, hb, qi: (bi, qi, hb)),
            pl.BlockSpec((1, s, w), lambda bi, hb, qi: (bi, 0, nblk + hb)),
            pl.BlockSpec((1, s, w), lambda bi, hb, qi: (bi, 0, 2 * nblk + hb)),
            pl.BlockSpec((1, tq, LANES), lambda bi, hb, qi: (bi, qi, 0)),
            pl.BlockSpec((1, 2 * ns, s), lambda bi, hb, qi: (bi, hb, 0)),
        ],
        out_specs=pl.BlockSpec((1, tq, w), lambda bi, hb, qi: (bi, qi, hb)),
        out_shape=jax.ShapeDtypeStruct((b, s, n_pairs * LANES), BF16),
        scratch_shapes=[
            pltpu.VMEM((s, ns * 2 * LANES), BF16),
            pltpu.VMEM((ns, 2 * tq, tq), F32),
            pltpu.VMEM((ns, 2 * tq, tq), BF16),
            pltpu.VMEM((ns, 2 * tq, LANES), F32),
            pltpu.VMEM((ns, 2 * tq, LANES), F32),
            pltpu.VMEM((ns, 2 * tq, LANES), F32),
        ],
        compiler_params=_params("parallel", "parallel", "arbitrary"),
        name="fox_attention",
    )(qkv, qkv, qkv, c, c_t)


def _swa_kernel(q_ref, kp_ref, kc_ref, vp_ref, vc_ref, bias_ref, sink_ref, o_ref, *, blk, n_blocks):
    qi = pl.program_id(2)
    n_pairs_here = q_ref.shape[2] // LANES
    k_all = jnp.concatenate([kp_ref[0], kc_ref[0]], axis=0)
    v_all = jnp.concatenate([vp_ref[0], vc_ref[0]], axis=0)
    lane = lax.broadcasted_iota(jnp.int32, v_all.shape, 1)
    one = jnp.ones(v_all.shape, BF16)
    v_heads = [jnp.where(lane < HEAD_DIM, v_all, one), jnp.where(lane < HEAD_DIM, one, v_all)]
    col = lax.broadcasted_iota(jnp.int32, (2 * blk, 2 * blk), 1)
    chains = [(r, a) for r in range(n_blocks) for a in range(n_pairs_here)]
    logits = []
    for r, a in chains:
        qm = _stack_heads(q_ref[0, r * blk:(r + 1) * blk, a * LANES:(a + 1) * LANES])
        s = _scores(qm, k_all[r * blk:(r + 2) * blk]) + bias_ref[2 * a:2 * a + 2].reshape(2 * blk, 2 * blk)
        if r == 0:
            s = jnp.where((col >= blk) | (qi > 0), s, -jnp.inf)
        logits.append(s)
    sums = []
    for (r, a), s in zip(chains, logits):
        sink = sink_ref[a]
        m = jnp.maximum(jnp.broadcast_to(jnp.max(s, axis=1, keepdims=True), (2 * blk, LANES)), sink)
        p = jnp.concatenate(
            [jnp.exp2(s[:, c * LANES:(c + 1) * LANES] - m).astype(BF16) for c in range(2 * blk // LANES)], axis=1)
        pv = jnp.concatenate(
            [jnp.dot(p[h * blk:(h + 1) * blk], v_heads[h][r * blk:(r + 2) * blk], preferred_element_type=F32)
             for h in (0, 1)], axis=0)
        sums.append((pv, jnp.exp2(sink - m)))
    for (r, a), (pv, sink_term) in zip(chains, sums):
        o = pv / (pltpu.roll(pv, HEAD_DIM, 1) + sink_term)
        o_ref[0, r * blk:(r + 1) * blk, a * LANES:(a + 1) * LANES] = _unstack_heads(o).astype(o_ref.dtype)


def _t5_bucket(dist):
    max_exact = REL_BUCKETS // 2
    d = jnp.maximum(dist, 1).astype(F32)
    large = max_exact + (jnp.log(d / max_exact) / math.log(REL_MAX_DIST / max_exact)
                         * (REL_BUCKETS - max_exact)).astype(jnp.int32)
    large = jnp.minimum(large, REL_BUCKETS - 1)
    return jnp.where(dist < max_exact, dist, large)


def swa_attention(proj, sinks, rel_bias, n_pairs):
    b, s, _ = proj.shape
    blk, nb = SWA_BLOCK, SWA_BLOCKS_PER_STEP
    tq = blk * nb
    group_pairs = n_pairs // SWA_KV_HEADS
    gw = group_pairs * LANES
    in_window = rel_bias.astype(F32)[_t5_bucket(jnp.arange(WINDOW))].T * LOG2E
    pad = jnp.full((in_window.shape[0], blk - 1), -jnp.inf, F32)
    by_key = jnp.concatenate(
        [jnp.full((in_window.shape[0], 2 * blk - WINDOW), -jnp.inf, F32), in_window[:, ::-1], pad], axis=1)
    bias = jnp.stack([by_key[:, blk - 1 - q:3 * blk - 1 - q] for q in range(blk)], axis=1)
    sink_rows = jnp.broadcast_to((sinks.astype(F32) * LOG2E).reshape(n_pairs, 2, 1, 1),
                                 (n_pairs, 2, blk, LANES)).reshape(n_pairs, 2 * blk, LANES)
    return pl.pallas_call(
        functools.partial(_swa_kernel, blk=blk, n_blocks=nb),
        grid=(b, SWA_KV_HEADS, s // tq),
        in_specs=[
            pl.BlockSpec((1, tq, gw), lambda bi, gk, qi: (bi, qi, gk)),
            pl.BlockSpec((1, blk, LANES), lambda bi, gk, qi: (bi, jnp.maximum(qi * nb - 1, 0), n_pairs + gk)),
            pl.BlockSpec((1, tq, LANES), lambda bi, gk, qi: (bi, qi, n_pairs + gk)),
            pl.BlockSpec((1, blk, LANES),
                         lambda bi, gk, qi: (bi, jnp.maximum(qi * nb - 1, 0), n_pairs + SWA_KV_HEADS + gk)),
            pl.BlockSpec((1, tq, LANES), lambda bi, gk, qi: (bi, qi, n_pairs + SWA_KV_HEADS + gk)),
            pl.BlockSpec((2 * group_pairs, blk, 2 * blk), lambda bi, gk, qi: (gk, 0, 0)),
            pl.BlockSpec((group_pairs, 2 * blk, LANES), lambda bi, gk, qi: (gk, 0, 0)),
        ],
        out_specs=pl.BlockSpec((1, tq, gw), lambda bi, gk, qi: (bi, qi, gk)),
        out_shape=jax.ShapeDtypeStruct((b, s, n_pairs * LANES), BF16),
        compiler_params=_params("parallel", "parallel", "arbitrary"),
        name="swa_attention",
    )(proj, proj, proj, proj, proj, bias, sink_rows)


def _duplicate_heads(w, n_heads):
    d = w.shape[0]
    w = w.reshape(d, n_heads, HEAD_DIM)
    return jnp.concatenate([w, w], axis=2).reshape(d, n_heads * LANES)


def _scale_q_columns(w_in, d):
    return jnp.concatenate([w_in[:, :d] * (SCALE * LOG2E), w_in[:, d:]], axis=1).astype(BF16)


def _swa_projection_weights(w_in, d):
    kw = SWA_KV_HEADS * HEAD_DIM
    w = _scale_q_columns(w_in, d)
    return jnp.concatenate(
        [w[:, :d], _duplicate_heads(w[:, d:d + kw], SWA_KV_HEADS), _duplicate_heads(w[:, d + kw:], SWA_KV_HEADS)],
        axis=1)


def kernel(x, p, attn_norm, mlp_norm, ple_norm, final_norm, w_in_sb, w_out_sb, w_in_fox, b_forget, w_out_fox,
           w_in_swa, sinks, w_out_swa, rel_bias, w_up, w_down, w_ple, w_ple_gate):
    b, s, d = x.shape
    depth = p.shape[0]
    t = b * s
    n_pairs = d // LANES
    h = x.reshape(t, d)
    for i in range(depth):
        kind = i % N_MIXERS
        j = i // N_MIXERS
        if kind == 0:
            qkv = norm_matmul(h, attn_norm[i], _scale_q_columns(w_in_sb[j], d))
            o = sb_attention(qkv.reshape(b, s, 3 * d), n_pairs)
            w_out = w_out_sb[j]
        elif kind == 1:
            w_in = w_in_fox[j]
            qkv = norm_matmul(h, attn_norm[i], _scale_q_columns(w_in[:, :3 * d], d))
            c, c_t = fox_cumulative_log_forget(h.reshape(b, s, d), attn_norm[i], w_in[:, 3 * d:].astype(BF16),
                                               b_forget[j])
            o = fox_attention(qkv.reshape(b, s, 3 * d), c, c_t, n_pairs)
            w_out = w_out_fox[j]
        else:
            proj = norm_matmul(h, attn_norm[i], _swa_projection_weights(w_in_swa[j], d))
            o = swa_attention(proj.reshape(b, s, 2 * d), sinks[j], rel_bias, n_pairs)
            w_out = w_out_swa[j]
        h = post_attention(o.reshape(t, d), h, p[i].reshape(t, p.shape[-1]), w_out.astype(BF16), mlp_norm[i],
                           w_up[i].astype(BF16), w_down[i].astype(BF16), ple_norm[i], w_ple_gate[i].astype(BF16),
                           w_ple[i].astype(BF16), final_norm, final=(i == depth - 1))
    return h.reshape(b, s, d)
```

```python
import functools
import math

import jax
import jax.numpy as jnp
from jax import lax
from jax.experimental import pallas as pl
from jax.experimental.pallas import tpu as pltpu

F32 = jnp.float32
BF16 = jnp.bfloat16

HEAD_DIM = 64
LANES = 128
N_MIXERS = 3
SWA_KV_HEADS = 4
WINDOW = 128
REL_BUCKETS = 32
REL_MAX_DIST = 128
EPS = 1e-6
SCALE = HEAD_DIM ** -0.5
LOG2E = math.log2(math.e)
SIGN_BIT = -2 ** 31
UNDERFLOW_LOG2 = -152.0

VMEM_LIMIT = 48 * 1024 * 1024
TOKEN_TILE = 512
FF_TILE = 1024
SB_TQ = 256
ROW_CHUNK = 32
SB_STREAMS = 4
FOX_TQ = 256
FOX_STREAMS = 4
SWA_BLOCK = 128
SWA_BLOCKS_PER_STEP = 4
GATE_TILE = 512


def _params(*sem):
    return pltpu.CompilerParams(dimension_semantics=sem, vmem_limit_bytes=VMEM_LIMIT)


def _rms(x, g):
    return x * lax.rsqrt(jnp.mean(x * x, axis=-1, keepdims=True) + EPS) * g


def _softplus2(z2):
    neg_abs = lax.bitcast_convert_type(lax.bitcast_convert_type(z2, jnp.int32) | jnp.int32(SIGN_BIT), F32)
    return jnp.maximum(z2, 0.0) + LOG2E * jnp.log(1.0 + jnp.exp2(neg_abs))


def _softplus(z):
    return jnp.maximum(z, 0.0) + jnp.log(1.0 + jnp.exp(-jnp.abs(z)))


def _norm_matmul_kernel(x_ref, g_ref, w_ref, o_ref):
    u = _rms(x_ref[...], g_ref[...]).astype(BF16)
    o_ref[...] = jnp.dot(u, w_ref[...], preferred_element_type=F32).astype(o_ref.dtype)


def norm_matmul(x, g, w):
    t, d = x.shape
    n = w.shape[1]
    tm = TOKEN_TILE
    return pl.pallas_call(
        _norm_matmul_kernel,
        grid=(t // tm,),
        in_specs=[
            pl.BlockSpec((tm, d), lambda i: (i, 0)),
            pl.BlockSpec((1, d), lambda i: (0, 0)),
            pl.BlockSpec((d, n), lambda i: (0, 0)),
        ],
        out_specs=pl.BlockSpec((tm, n), lambda i: (i, 0)),
        out_shape=jax.ShapeDtypeStruct((t, n), BF16),
        compiler_params=_params("parallel"),
        name="norm_matmul",
    )(x, g.reshape(1, d), w)


def _post_attention_kernel(o_ref, h_ref, p_ref, wo_ref, gm_ref, wu_ref, wd_ref, gp_ref, wg_ref, wp_ref, fn_ref,
                           out_ref, u_ref, *, final):
    f = pl.program_id(1)

    @pl.when(f == 0)
    def _():
        h = h_ref[...] + jnp.dot(o_ref[...], wo_ref[...], preferred_element_type=F32)
        u_ref[...] = _rms(h, gm_ref[...]).astype(BF16)
        out_ref[...] = h

    a = jnp.maximum(jnp.dot(u_ref[...], wu_ref[...], preferred_element_type=F32), 0.0)
    out_ref[...] += jnp.dot((a * a).astype(BF16), wd_ref[...], preferred_element_type=F32)

    @pl.when(f == pl.num_programs(1) - 1)
    def _():
        h = out_ref[...]
        u = _rms(h, gp_ref[...]).astype(BF16)
        gate = 1.0 / (1.0 + jnp.exp(-jnp.dot(u, wg_ref[...], preferred_element_type=F32)))
        e = jnp.dot(p_ref[...].astype(BF16), wp_ref[...], preferred_element_type=F32)
        hn = h + e * gate
        if final:
            hn = _rms(hn, fn_ref[...])
        out_ref[...] = hn


def post_attention(o, h, p, w_out, g_mlp, w_up, w_down, g_ple, w_gate, w_ple, final_g, final):
    t, d = h.shape
    f = w_up.shape[1]
    pd = p.shape[1]
    tm, tf = TOKEN_TILE, FF_TILE
    row = lambda i, j: (i, 0)
    fixed = lambda i, j: (0, 0)
    return pl.pallas_call(
        functools.partial(_post_attention_kernel, final=final),
        grid=(t // tm, f // tf),
        in_specs=[
            pl.BlockSpec((tm, d), row),
            pl.BlockSpec((tm, d), row),
            pl.BlockSpec((tm, pd), row),
            pl.BlockSpec((d, d), fixed),
            pl.BlockSpec((1, d), fixed),
            pl.BlockSpec((d, tf), lambda i, j: (0, j)),
            pl.BlockSpec((tf, d), lambda i, j: (j, 0)),
            pl.BlockSpec((1, d), fixed),
            pl.BlockSpec((d, d), fixed),
            pl.BlockSpec((pd, d), fixed),
            pl.BlockSpec((1, d), fixed),
        ],
        out_specs=pl.BlockSpec((tm, d), row),
        out_shape=jax.ShapeDtypeStruct((t, d), F32),
        scratch_shapes=[pltpu.VMEM((tm, d), BF16)],
        compiler_params=_params("parallel", "arbitrary"),
        name="post_attention",
    )(o, h, p, w_out, g_mlp.reshape(1, d), w_up, w_down, g_ple.reshape(1, d), w_gate, w_ple, final_g.reshape(1, d))


def _stack_heads(q):
    tq = q.shape[0]
    lane = lax.broadcasted_iota(jnp.int32, (tq, LANES), 1)
    zero = jnp.zeros_like(q)
    return jnp.concatenate(
        [jnp.where(lane < HEAD_DIM, q, zero), jnp.where(lane >= HEAD_DIM, q, zero)], axis=0)


def _unstack_heads(acc):
    tq = acc.shape[0] // 2
    lane = lax.broadcasted_iota(jnp.int32, (tq, LANES), 1)
    return jnp.where(lane < HEAD_DIM, acc[:tq], acc[tq:])


def _scores(qm, kb):
    return lax.dot_general(qm, kb, (((1,), (1,)), ((), ())), preferred_element_type=F32)


def _sb_kernel(q_ref, k_ref, v_ref, u_ref, o_ref, z_ref, hi_ref, zms_ref, w_ref, carry_ref, acc_ref,
               *, tq, n_streams):
    qi = pl.program_id(2)
    lanes = [slice(p * LANES, (p + 1) * LANES) for p in range(n_streams)]
    qms = [_stack_heads(q_ref[0, :, lanes[p]]) for p in range(n_streams)]
    row_c = lax.broadcasted_iota(jnp.int32, (ROW_CHUNK, tq), 0)
    col_c = lax.broadcasted_iota(jnp.int32, (ROW_CHUNK, tq), 1)
    lane_c = lax.broadcasted_iota(jnp.int32, (ROW_CHUNK, LANES), 1)
    umat = u_ref[...]

    def scores(p, g):
        start = pl.multiple_of(g * tq, tq)
        return _scores(qms[p], k_ref[0, pl.ds(start, tq), lanes[p]])

    def trip(g, first):
        masked = first
        v_start = pl.multiple_of(g * tq, tq)
        cts = []
        for p in range(n_streams):
            for r in range(0, 2 * tq, ROW_CHUNK):
                z = z_ref[p, r:r + ROW_CHUNK]
                sp = _softplus2(z)
                if masked:
                    sp = jnp.where(col_c < row_c + (r % tq), sp, 0.0)
                hi_ref[p, r:r + ROW_CHUNK] = sp.astype(BF16)
                zms_ref[p, r:r + ROW_CHUNK] = z - sp
            cts.append(jnp.dot(hi_ref[p], umat, preferred_element_type=F32))
            z_ref[p] = scores(p, jnp.maximum(g - 1, 0))
        pvs = []
        top = None
        for p in range(n_streams):
            for r in range(0, 2 * tq, ROW_CHUNK):
                ct = cts[p][r:r + ROW_CHUNK]
                ct_last = ct[:, tq - LANES:]
                ct_keys = jnp.concatenate(
                    [ct[:, :tq - LANES], jnp.where(lane_c == LANES - 1, 0.0, ct_last)], axis=1)
                e = zms_ref[p, r:r + ROW_CHUNK] + ct_keys
                if masked:
                    e = jnp.where(col_c < row_c + (r % tq), e, -jnp.inf)
                total = jnp.broadcast_to(ct_last[:, LANES - 1:], (ROW_CHUNK, LANES))
                if first:
                    cols = [e[:, c * LANES:(c + 1) * LANES] for c in range(tq // LANES)]
                    carry_new = total
                else:
                    carry = carry_ref[p, r:r + ROW_CHUNK]
                    cols = [e[:, c * LANES:(c + 1) * LANES] + carry for c in range(tq // LANES)]
                    carry_new = carry + total
                carry_ref[p, r:r + ROW_CHUNK] = carry_new
                top = carry_new if top is None else jnp.maximum(top, carry_new)
                w_ref[p, r:r + ROW_CHUNK] = jnp.concatenate([jnp.exp2(c).astype(BF16) for c in cols], axis=1)
            pvs.append(jnp.dot(w_ref[p], v_ref[0, pl.ds(v_start, tq), lanes[p]], preferred_element_type=F32))
        for p in range(n_streams):
            if first:
                acc_ref[p] = pvs[p]
            else:
                acc_ref[p] += pvs[p]
        return jnp.max(top)

    for p in range(n_streams):
        z_ref[p] = scores(p, qi)
    top0 = trip(qi, True)

    def keep_going(state):
        g, top = state
        return (g >= 0) & (top > UNDERFLOW_LOG2)

    def body(state):
        g, _ = state
        return g - 1, trip(g, False)

    lax.while_loop(keep_going, body, (qi - 1, top0))
    for p in range(n_streams):
        o_ref[0, :, lanes[p]] = _unstack_heads(acc_ref[p]).astype(o_ref.dtype)


def _later_keys_matrix(tq):
    j = jnp.arange(tq)[:, None]
    s = jnp.arange(tq)[None, :]
    return -((j > s) | (s == tq - 1)).astype(BF16)


def sb_attention(qkv, n_pairs):
    b, s, _ = qkv.shape
    tq, ns = SB_TQ, SB_STREAMS
    w = ns * LANES
    nblk = n_pairs // ns
    return pl.pallas_call(
        functools.partial(_sb_kernel, tq=tq, n_streams=ns),
        grid=(b, nblk, s // tq),
        in_specs=[
            pl.BlockSpec((1, tq, w), lambda bi, hb, qi: (bi, qi, hb)),
            pl.BlockSpec((1, s, w), lambda bi, hb, qi: (bi, 0, nblk + hb)),
            pl.BlockSpec((1, s, w), lambda bi, hb, qi: (bi, 0, 2 * nblk + hb)),
            pl.BlockSpec((tq, tq), lambda bi, hb, qi: (0, 0)),
        ],
        out_specs=pl.BlockSpec((1, tq, w), lambda bi, hb, qi: (bi, qi, hb)),
        out_shape=jax.ShapeDtypeStruct((b, s, n_pairs * LANES), BF16),
        scratch_shapes=[
            pltpu.VMEM((ns, 2 * tq, tq), F32),
            pltpu.VMEM((ns, 2 * tq, tq), BF16),
            pltpu.VMEM((ns, 2 * tq, tq), F32),
            pltpu.VMEM((ns, 2 * tq, tq), BF16),
            pltpu.VMEM((ns, 2 * tq, LANES), F32),
            pltpu.VMEM((ns, 2 * tq, LANES), F32),
        ],
        compiler_params=_params("parallel", "parallel", "arbitrary"),
        name="sb_attention",
    )(qkv, qkv, qkv, _later_keys_matrix(tq))


def _fox_gate_kernel(x_ref, g_ref, w_ref, b_ref, tri_ref, c_ref, ct_ref, carry_ref):
    @pl.when(pl.program_id(1) == 0)
    def _():
        carry_ref[...] = jnp.zeros_like(carry_ref)

    tm = x_ref.shape[1]
    u = _rms(x_ref[0], g_ref[...]).astype(BF16)
    logit = jnp.dot(u, w_ref[...], preferred_element_type=F32) + b_ref[...]
    log_f = -_softplus(-logit)
    hi = log_f.astype(BF16)
    r = log_f - hi.astype(F32)
    mid = r.astype(BF16)
    lo = (r - mid.astype(F32)).astype(BF16)
    tri = tri_ref[...]
    c = (jnp.dot(tri, hi, preferred_element_type=F32)
         + jnp.dot(tri, mid, preferred_element_type=F32)
         + jnp.dot(tri, lo, preferred_element_type=F32)) + carry_ref[0:1, :]
    c2 = c * LOG2E
    c_ref[0] = c2
    ct_ref[0] = c2.T
    carry_ref[...] = jnp.broadcast_to(c[tm - 1:tm, :], carry_ref.shape)


def fox_cumulative_log_forget(h, g, w_f, b_f):
    b, s, d = h.shape
    nh = w_f.shape[1]
    tm = GATE_TILE
    w_pad = jnp.zeros((d, LANES), BF16).at[:, :nh].set(w_f)
    b_pad = jnp.zeros((1, LANES), F32).at[0, :nh].set(b_f)
    tri = (jnp.arange(tm)[:, None] >= jnp.arange(tm)[None, :]).astype(BF16)
    return pl.pallas_call(
        _fox_gate_kernel,
        grid=(b, s // tm),
        in_specs=[
            pl.BlockSpec((1, tm, d), lambda bi, si: (bi, si, 0)),
            pl.BlockSpec((1, d), lambda bi, si: (0, 0)),
            pl.BlockSpec((d, LANES), lambda bi, si: (0, 0)),
            pl.BlockSpec((1, LANES), lambda bi, si: (0, 0)),
            pl.BlockSpec((tm, tm), lambda bi, si: (0, 0)),
        ],
        out_specs=[pl.BlockSpec((1, tm, LANES), lambda bi, si: (bi, si, 0)),
                   pl.BlockSpec((1, LANES, tm), lambda bi, si: (bi, 0, si))],
        out_shape=[jax.ShapeDtypeStruct((b, s, LANES), F32), jax.ShapeDtypeStruct((b, LANES, s), F32)],
        scratch_shapes=[pltpu.VMEM((8, LANES), F32)],
        compiler_params=_params("parallel", "arbitrary"),
        name="fox_gate",
    )(h, g.reshape(1, d), w_pad, b_pad, tri)


def _fox_kernel(q_ref, k_ref, v_ref, cq_ref, ck_ref, o_ref, vext_ref, s_ref, w_ref, m_ref, alpha_ref, acc_ref,
                *, tq, n_streams):
    hb = pl.program_id(1)
    qi = pl.program_id(2)
    lanes = [slice(p * LANES, (p + 1) * LANES) for p in range(n_streams)]

    @pl.when(qi == 0)
    def _():
        lane = lax.broadcasted_iota(jnp.int32, (vext_ref.shape[0], LANES), 1)
        one = jnp.ones((vext_ref.shape[0], LANES), BF16)
        for p in range(n_streams):
            v = v_ref[0, :, lanes[p]]
            vext_ref[:, 2 * p * LANES:2 * (p + 1) * LANES] = jnp.concatenate(
                [jnp.where(lane < HEAD_DIM, v, one), jnp.where(lane < HEAD_DIM, one, v)], axis=1)

    qms = [_stack_heads(q_ref[0, :, lanes[p]]) for p in range(n_streams)]
    row_c = lax.broadcasted_iota(jnp.int32, (ROW_CHUNK, tq), 0)
    col_c = lax.broadcasted_iota(jnp.int32, (ROW_CHUNK, tq), 1)

    cq = cq_ref[0]
    head_lane = lax.broadcasted_iota(jnp.int32, cq.shape, 1)
    c_rows = [jnp.concatenate(
        [jnp.broadcast_to(
            jnp.sum(jnp.where(head_lane == 2 * (hb * n_streams + p) + a, cq, 0.0), axis=1, keepdims=True),
            (tq, LANES)) for a in (0, 1)], axis=0) for p in range(n_streams)]

    def scores(p, g):
        start = pl.multiple_of(g * tq, tq)
        z = _scores(qms[p], k_ref[0, pl.ds(start, tq), lanes[p]])
        cols = []
        for c in range(tq // LANES):
            ck = jnp.concatenate(
                [jnp.broadcast_to(ck_ref[0, 2 * p + a:2 * p + a + 1, pl.ds(start + c * LANES, LANES)], (tq, LANES))
                 for a in (0, 1)], axis=0)
            cols.append(z[:, c * LANES:(c + 1) * LANES] + (c_rows[p] - ck))
        return jnp.concatenate(cols, axis=1)

    def value_matmul(p, g):
        start = pl.multiple_of(g * tq, tq)
        halves = []
        for a in (0, 1):
            v_cols = slice((2 * p + a) * LANES, (2 * p + a + 1) * LANES)
            halves.append(jnp.dot(w_ref[p, a * tq:(a + 1) * tq], vext_ref[pl.ds(start, tq), v_cols],
                                  preferred_element_type=F32))
        return jnp.concatenate(halves, axis=0)

    def trip(g, first):
        if not first:
            pvs = [value_matmul(p, g + 1) for p in range(n_streams)]
        for p in range(n_streams):
            for r in range(0, 2 * tq, ROW_CHUNK):
                rows = slice(r, r + ROW_CHUNK)
                s = s_ref[p, rows]
                if first:
                    s = jnp.where(col_c <= row_c + (r % tq), s, -jnp.inf)
                    m_new = jnp.broadcast_to(jnp.max(s, axis=1, keepdims=True), (ROW_CHUNK, LANES))
                    acc_ref[p, rows] = jnp.zeros((ROW_CHUNK, LANES), F32)
                    alpha_ref[p, rows] = jnp.zeros((ROW_CHUNK, LANES), F32)
                else:
                    acc_ref[p, rows] = acc_ref[p, rows] * alpha_ref[p, rows] + pvs[p][rows]
                    m_old = m_ref[p, rows]
                    m_new = jnp.maximum(m_old, jnp.broadcast_to(jnp.max(s, axis=1, keepdims=True), (ROW_CHUNK, LANES)))
                    alpha_ref[p, rows] = jnp.exp2(m_old - m_new)
                m_ref[p, rows] = m_new
                w_ref[p, rows] = jnp.concatenate(
                    [jnp.exp2(s[:, c * LANES:(c + 1) * LANES] - m_new).astype(BF16) for c in range(tq // LANES)],
                    axis=1)
            s_ref[p] = scores(p, jnp.maximum(g - 1, 0))

    for p in range(n_streams):
        s_ref[p] = scores(p, qi)
    trip(qi, True)

    def body(i, _):
        trip(qi - 1 - i, False)
        return 0

    lax.fori_loop(0, qi, body, 0)
    for p in range(n_streams):
        acc = acc_ref[p] * alpha_ref[p] + value_matmul(p, 0)
        o_ref[0, :, lanes[p]] = _unstack_heads(acc / pltpu.roll(acc, HEAD_DIM, 1)).astype(o_ref.dtype)


def fox_attention(qkv, c, c_t, n_pairs):
    b, s, _ = qkv.shape
    tq, ns = FOX_TQ, FOX_STREAMS
    w = ns * LANES
    nblk = n_pairs // ns
    return pl.pallas_call(
        functools.partial(_fox_kernel, tq=tq, n_streams=ns),
        grid=(b, nblk, s // tq),
        in_specs=[
            pl.BlockSpec((1, tq, w), lambda bi, hb, qi: (bi, qi, hb)),
            pl.BlockSpec((1, s, w), lambda bi, hb, qi: (bi, 0, nblk + hb)),
            pl.BlockSpec((1, s, w), lambda bi, hb, qi: (bi, 0, 2 * nblk + hb)),
            pl.BlockSpec((1, tq, LANES), lambda bi, hb, qi: (bi, qi, 0)),
            pl.BlockSpec((1, 2 * ns, s), lambda bi, hb, qi: (bi, hb, 0)),
        ],
        out_specs=pl.BlockSpec((1, tq, w), lambda bi, hb, qi: (bi, qi, hb)),
        out_shape=jax.ShapeDtypeStruct((b, s, n_pairs * LANES), BF16),
        scratch_shapes=[
            pltpu.VMEM((s, ns * 2 * LANES), BF16),
            pltpu.VMEM((ns, 2 * tq, tq), F32),
            pltpu.VMEM((ns, 2 * tq, tq), BF16),
            pltpu.VMEM((ns, 2 * tq, LANES), F32),
            pltpu.VMEM((ns, 2 * tq, LANES), F32),
            pltpu.VMEM((ns, 2 * tq, LANES), F32),
        ],
        compiler_params=_params("parallel", "parallel", "arbitrary"),
        name="fox_attention",
    )(qkv, qkv, qkv, c, c_t)


def _swa_kernel(q_ref, kp_ref, kc_ref, vp_ref, vc_ref, bias_ref, sink_ref, o_ref, *, blk, n_blocks):
    qi = pl.program_id(2)
    n_pairs_here = q_ref.shape[2] // LANES
    k_all = jnp.concatenate([kp_ref[0], kc_ref[0]], axis=0)
    v_all = jnp.concatenate([vp_ref[0], vc_ref[0]], axis=0)
    lane = lax.broadcasted_iota(jnp.int32, v_all.shape, 1)
    one = jnp.ones(v_all.shape, BF16)
    v_heads = [jnp.where(lane < HEAD_DIM, v_all, one), jnp.where(lane < HEAD_DIM, one, v_all)]
    col = lax.broadcasted_iota(jnp.int32, (2 * blk, 2 * blk), 1)
    chains = [(r, a) for r in range(n_blocks) for a in range(n_pairs_here)]
    logits = []
    for r, a in chains:
        qm = _stack_heads(q_ref[0, r * blk:(r + 1) * blk, a * LANES:(a + 1) * LANES])
        s = _scores(qm, k_all[r * blk:(r + 2) * blk]) + bias_ref[2 * a:2 * a + 2].reshape(2 * blk, 2 * blk)
        if r == 0:
            s = jnp.where((col >= blk) | (qi > 0), s, -jnp.inf)
        logits.append(s)
    sums = []
    for (r, a), s in zip(chains, logits):
        sink = sink_ref[a]
        m = jnp.maximum(jnp.broadcast_to(jnp.max(s, axis=1, keepdims=True), (2 * blk, LANES)), sink)
        p = jnp.concatenate(
            [jnp.exp2(s[:, c * LANES:(c + 1) * LANES] - m).astype(BF16) for c in range(2 * blk // LANES)], axis=1)
        pv = jnp.concatenate(
            [jnp.dot(p[h * blk:(h + 1) * blk], v_heads[h][r * blk:(r + 2) * blk], preferred_element_type=F32)
             for h in (0, 1)], axis=0)
        sums.append((pv, jnp.exp2(sink - m)))
    for (r, a), (pv, sink_term) in zip(chains, sums):
        o = pv / (pltpu.roll(pv, HEAD_DIM, 1) + sink_term)
        o_ref[0, r * blk:(r + 1) * blk, a * LANES:(a + 1) * LANES] = _unstack_heads(o).astype(o_ref.dtype)


def _t5_bucket(dist):
    max_exact = REL_BUCKETS // 2
    d = jnp.maximum(dist, 1).astype(F32)
    large = max_exact + (jnp.log(d / max_exact) / math.log(REL_MAX_DIST / max_exact)
                         * (REL_BUCKETS - max_exact)).astype(jnp.int32)
    large = jnp.minimum(large, REL_BUCKETS - 1)
    return jnp.where(dist < max_exact, dist, large)


def swa_attention(proj, sinks, rel_bias, n_pairs):
    b, s, _ = proj.shape
    blk, nb = SWA_BLOCK, SWA_BLOCKS_PER_STEP
    tq = blk * nb
    group_pairs = n_pairs // SWA_KV_HEADS
    gw = group_pairs * LANES
    in_window = rel_bias.astype(F32)[_t5_bucket(jnp.arange(WINDOW))].T * LOG2E
    pad = jnp.full((in_window.shape[0], blk - 1), -jnp.inf, F32)
    by_key = jnp.concatenate(
        [jnp.full((in_window.shape[0], 2 * blk - WINDOW), -jnp.inf, F32), in_window[:, ::-1], pad], axis=1)
    bias = jnp.stack([by_key[:, blk - 1 - q:3 * blk - 1 - q] for q in range(blk)], axis=1)
    sink_rows = jnp.broadcast_to((sinks.astype(F32) * LOG2E).reshape(n_pairs, 2, 1, 1),
                                 (n_pairs, 2, blk, LANES)).reshape(n_pairs, 2 * blk, LANES)
    return pl.pallas_call(
        functools.partial(_swa_kernel, blk=blk, n_blocks=nb),
        grid=(b, SWA_KV_HEADS, s // tq),
        in_specs=[
            pl.BlockSpec((1, tq, gw), lambda bi, gk, qi: (bi, qi, gk)),
            pl.BlockSpec((1, blk, LANES), lambda bi, gk, qi: (bi, jnp.maximum(qi * nb - 1, 0), n_pairs + gk)),
            pl.BlockSpec((1, tq, LANES), lambda bi, gk, qi: (bi, qi, n_pairs + gk)),
            pl.BlockSpec((1, blk, LANES),
                         lambda bi, gk, qi: (bi, jnp.maximum(qi * nb - 1, 0), n_pairs + SWA_KV_HEADS + gk)),
            pl.BlockSpec((1, tq, LANES), lambda bi, gk, qi: (bi, qi, n_pairs + SWA_KV_HEADS + gk)),
            pl.BlockSpec((2 * group_pairs, blk, 2 * blk), lambda bi, gk, qi: (gk, 0, 0)),
            pl.BlockSpec((group_pairs, 2 * blk, LANES), lambda bi, gk, qi: (gk, 0, 0)),
        ],
        out_specs=pl.BlockSpec((1, tq, gw), lambda bi, gk, qi: (bi, qi, gk)),
        out_shape=jax.ShapeDtypeStruct((b, s, n_pairs * LANES), BF16),
        compiler_params=_params("parallel", "parallel", "arbitrary"),
        name="swa_attention",
    )(proj, proj, proj, proj, proj, bias, sink_rows)


def _duplicate_heads(w, n_heads):
    d = w.shape[0]
    w = w.reshape(d, n_heads, HEAD_DIM)
    return jnp.concatenate([w, w], axis=2).reshape(d, n_heads * LANES)


def _scale_q_columns(w_in, d):
    return jnp.concatenate([w_in[:, :d] * (SCALE * LOG2E), w_in[:, d:]], axis=1).astype(BF16)


def _swa_projection_weights(w_in, d):
    kw = SWA_KV_HEADS * HEAD_DIM
    w = _scale_q_columns(w_in, d)
    return jnp.concatenate(
        [w[:, :d], _duplicate_heads(w[:, d:d + kw], SWA_KV_HEADS), _duplicate_heads(w[:, d + kw:], SWA_KV_HEADS)],
        axis=1)


def kernel(x, p, attn_norm, mlp_norm, ple_norm, final_norm, w_in_sb, w_out_sb, w_in_fox, b_forget, w_out_fox,
           w_in_swa, sinks, w_out_swa, rel_bias, w_up, w_down, w_ple, w_ple_gate):
    b, s, d = x.shape
    depth = p.shape[0]
    t = b * s
    n_pairs = d // LANES
    h = x.reshape(t, d)
    for i in range(depth):
        kind = i % N_MIXERS
        j = i // N_MIXERS
        if kind == 0:
            qkv = norm_matmul(h, attn_norm[i], _scale_q_columns(w_in_sb[j], d))
            o = sb_attention(qkv.reshape(b, s, 3 * d), n_pairs)
            w_out = w_out_sb[j]
        elif kind == 1:
            w_in = w_in_fox[j]
            qkv = norm_matmul(h, attn_norm[i], _scale_q_columns(w_in[:, :3 * d], d))
            c, c_t = fox_cumulative_log_forget(h.reshape(b, s, d), attn_norm[i], w_in[:, 3 * d:].astype(BF16),
                                               b_forget[j])
            o = fox_attention(qkv.reshape(b, s, 3 * d), c, c_t, n_pairs)
            w_out = w_out_fox[j]
        else:
            proj = norm_matmul(h, attn_norm[i], _swa_projection_weights(w_in_swa[j], d))
            o = swa_attention(proj.reshape(b, s, 2 * d), sinks[j], rel_bias, n_pairs)
            w_out = w_out_swa[j]
        h = post_attention(o.reshape(t, d), h, p[i].reshape(t, p.shape[-1]), w_out.astype(BF16), mlp_norm[i],
                           w_up[i].astype(BF16), w_down[i].astype(BF16), ple_norm[i], w_ple_gate[i].astype(BF16),
                           w_ple[i].astype(BF16), final_norm, final=(i == depth - 1))
    return h.reshape(b, s, d)
```

```python
import functools
import math

import jax
import jax.numpy as jnp
from jax import lax
from jax.experimental import pallas as pl
from jax.experimental.pallas import tpu as pltpu

F32 = jnp.float32
BF16 = jnp.bfloat16

HEAD_DIM = 64
LANES = 128
N_MIXERS = 3
SWA_KV_HEADS = 4
WINDOW = 128
REL_BUCKETS = 32
REL_MAX_DIST = 128
EPS = 1e-6
SCALE = HEAD_DIM ** -0.5
LOG2E = math.log2(math.e)
SIGN_BIT = -2 ** 31
UNDERFLOW_LOG2 = -152.0

VMEM_LIMIT = 48 * 1024 * 1024
TOKEN_TILE = 512
FF_TILE = 2048
SB_TQ = 256
ROW_CHUNK = 32
SB_STREAMS = 4
FOX_TQ = 256
FOX_STREAMS = 4
SWA_BLOCK = 128
SWA_BLOCKS_PER_STEP = 8
GATE_TILE = 512


def _params(*sem):
    return pltpu.CompilerParams(dimension_semantics=sem, vmem_limit_bytes=VMEM_LIMIT)


def _rms(x, g):
    return x * lax.rsqrt(jnp.mean(x * x, axis=-1, keepdims=True) + EPS) * g


def _softplus2(z2):
    neg_abs = lax.bitcast_convert_type(lax.bitcast_convert_type(z2, jnp.int32) | jnp.int32(SIGN_BIT), F32)
    return jnp.maximum(z2, 0.0) + LOG2E * jnp.log(1.0 + jnp.exp2(neg_abs))


def _softplus(z):
    return jnp.maximum(z, 0.0) + jnp.log(1.0 + jnp.exp(-jnp.abs(z)))


def _norm_matmul_kernel(x_ref, g_ref, w_ref, o_ref):
    u = _rms(x_ref[...], g_ref[...]).astype(BF16)
    o_ref[...] = jnp.dot(u, w_ref[...], preferred_element_type=F32).astype(o_ref.dtype)


def norm_matmul(x, g, w):
    t, d = x.shape
    n = w.shape[1]
    tm = TOKEN_TILE
    return pl.pallas_call(
        _norm_matmul_kernel,
        grid=(t // tm,),
        in_specs=[
            pl.BlockSpec((tm, d), lambda i: (i, 0)),
            pl.BlockSpec((1, d), lambda i: (0, 0)),
            pl.BlockSpec((d, n), lambda i: (0, 0)),
        ],
        out_specs=pl.BlockSpec((tm, n), lambda i: (i, 0)),
        out_shape=jax.ShapeDtypeStruct((t, n), BF16),
        compiler_params=_params("parallel"),
        name="norm_matmul",
    )(x, g.reshape(1, d), w)


def _post_attention_kernel(o_ref, h_ref, p_ref, wo_ref, gm_ref, wu_ref, wd_ref, gp_ref, wg_ref, wp_ref, fn_ref,
                           out_ref, u_ref, *, final):
    f = pl.program_id(1)

    @pl.when(f == 0)
    def _():
        h = h_ref[...] + jnp.dot(o_ref[...], wo_ref[...], preferred_element_type=F32)
        u_ref[...] = _rms(h, gm_ref[...]).astype(BF16)
        out_ref[...] = h

    a = jnp.maximum(jnp.dot(u_ref[...], wu_ref[...], preferred_element_type=F32), 0.0)
    out_ref[...] += jnp.dot((a * a).astype(BF16), wd_ref[...], preferred_element_type=F32)

    @pl.when(f == pl.num_programs(1) - 1)
    def _():
        h = out_ref[...]
        u = _rms(h, gp_ref[...]).astype(BF16)
        gate = 1.0 / (1.0 + jnp.exp(-jnp.dot(u, wg_ref[...], preferred_element_type=F32)))
        e = jnp.dot(p_ref[...].astype(BF16), wp_ref[...], preferred_element_type=F32)
        hn = h + e * gate
        if final:
            hn = _rms(hn, fn_ref[...])
        out_ref[...] = hn


def post_attention(o, h, p, w_out, g_mlp, w_up, w_down, g_ple, w_gate, w_ple, final_g, final):
    t, d = h.shape
    f = w_up.shape[1]
    pd = p.shape[1]
    tm, tf = TOKEN_TILE, FF_TILE
    row = lambda i, j: (i, 0)
    fixed = lambda i, j: (0, 0)
    return pl.pallas_call(
        functools.partial(_post_attention_kernel, final=final),
        grid=(t // tm, f // tf),
        in_specs=[
            pl.BlockSpec((tm, d), row),
            pl.BlockSpec((tm, d), row),
            pl.BlockSpec((tm, pd), row),
            pl.BlockSpec((d, d), fixed),
            pl.BlockSpec((1, d), fixed),
            pl.BlockSpec((d, tf), lambda i, j: (0, j)),
            pl.BlockSpec((tf, d), lambda i, j: (j, 0)),
            pl.BlockSpec((1, d), fixed),
            pl.BlockSpec((d, d), fixed),
            pl.BlockSpec((pd, d), fixed),
            pl.BlockSpec((1, d), fixed),
        ],
        out_specs=pl.BlockSpec((tm, d), row),
        out_shape=jax.ShapeDtypeStruct((t, d), F32),
        scratch_shapes=[pltpu.VMEM((tm, d), BF16)],
        compiler_params=_params("parallel", "arbitrary"),
        name="post_attention",
    )(o, h, p, w_out, g_mlp.reshape(1, d), w_up, w_down, g_ple.reshape(1, d), w_gate, w_ple, final_g.reshape(1, d))


def _stack_heads(q):
    tq = q.shape[0]
    lane = lax.broadcasted_iota(jnp.int32, (tq, LANES), 1)
    zero = jnp.zeros_like(q)
    return jnp.concatenate(
        [jnp.where(lane < HEAD_DIM, q, zero), jnp.where(lane >= HEAD_DIM, q, zero)], axis=0)


def _unstack_heads(acc):
    tq = acc.shape[0] // 2
    lane = lax.broadcasted_iota(jnp.int32, (tq, LANES), 1)
    return jnp.where(lane < HEAD_DIM, acc[:tq], acc[tq:])


def _scores(qm, kb):
    return lax.dot_general(qm, kb, (((1,), (1,)), ((), ())), preferred_element_type=F32)


def _sb_kernel(q_ref, k_ref, v_ref, u_ref, o_ref, z_ref, hi_ref, zms_ref, w_ref, carry_ref, acc_ref,
               *, tq, n_streams):
    qi = pl.program_id(2)
    lanes = [slice(p * LANES, (p + 1) * LANES) for p in range(n_streams)]
    qms = [_stack_heads(q_ref[0, :, lanes[p]]) for p in range(n_streams)]
    row_c = lax.broadcasted_iota(jnp.int32, (ROW_CHUNK, tq), 0)
    col_c = lax.broadcasted_iota(jnp.int32, (ROW_CHUNK, tq), 1)
    lane_c = lax.broadcasted_iota(jnp.int32, (ROW_CHUNK, LANES), 1)
    umat = u_ref[...]

    def scores(p, g):
        start = pl.multiple_of(g * tq, tq)
        return _scores(qms[p], k_ref[0, pl.ds(start, tq), lanes[p]])

    def trip(g, first):
        masked = first
        v_start = pl.multiple_of(g * tq, tq)
        cts = []
        for p in range(n_streams):
            for r in range(0, 2 * tq, ROW_CHUNK):
                z = z_ref[p, r:r + ROW_CHUNK]
                sp = _softplus2(z)
                if masked:
                    sp = jnp.where(col_c < row_c + (r % tq), sp, 0.0)
                hi_ref[p, r:r + ROW_CHUNK] = sp.astype(BF16)
                zms_ref[p, r:r + ROW_CHUNK] = z - sp
            cts.append(jnp.dot(hi_ref[p], umat, preferred_element_type=F32))
            z_ref[p] = scores(p, jnp.maximum(g - 1, 0))
        pvs = []
        top = None
        for p in range(n_streams):
            for r in range(0, 2 * tq, ROW_CHUNK):
                ct = cts[p][r:r + ROW_CHUNK]
                ct_last = ct[:, tq - LANES:]
                ct_keys = jnp.concatenate(
                    [ct[:, :tq - LANES], jnp.where(lane_c == LANES - 1, 0.0, ct_last)], axis=1)
                e = zms_ref[p, r:r + ROW_CHUNK] + ct_keys
                if masked:
                    e = jnp.where(col_c < row_c + (r % tq), e, -jnp.inf)
                total = jnp.broadcast_to(ct_last[:, LANES - 1:], (ROW_CHUNK, LANES))
                if first:
                    cols = [e[:, c * LANES:(c + 1) * LANES] for c in range(tq // LANES)]
                    carry_new = total
                else:
                    carry = carry_ref[p, r:r + ROW_CHUNK]
                    cols = [e[:, c * LANES:(c + 1) * LANES] + carry for c in range(tq // LANES)]
                    carry_new = carry + total
                carry_ref[p, r:r + ROW_CHUNK] = carry_new
                top = carry_new if top is None else jnp.maximum(top, carry_new)
                w_ref[p, r:r + ROW_CHUNK] = jnp.concatenate([jnp.exp2(c).astype(BF16) for c in cols], axis=1)
            pvs.append(jnp.dot(w_ref[p], v_ref[0, pl.ds(v_start, tq), lanes[p]], preferred_element_type=F32))
        for p in range(n_streams):
            if first:
                acc_ref[p] = pvs[p]
            else:
                acc_ref[p] += pvs[p]
        return jnp.max(top)

    for p in range(n_streams):
        z_ref[p] = scores(p, qi)
    top0 = trip(qi, True)

    def keep_going(state):
        g, top = state
        return (g >= 0) & (top > UNDERFLOW_LOG2)

    def body(state):
        g, _ = state
        return g - 1, trip(g, False)

    lax.while_loop(keep_going, body, (qi - 1, top0))
    for p in range(n_streams):
        o_ref[0, :, lanes[p]] = _unstack_heads(acc_ref[p]).astype(o_ref.dtype)


def _later_keys_matrix(tq):
    j = jnp.arange(tq)[:, None]
    s = jnp.arange(tq)[None, :]
    return -((j > s) | (s == tq - 1)).astype(BF16)


def sb_attention(qkv, n_pairs):
    b, s, _ = qkv.shape
    tq, ns = SB_TQ, SB_STREAMS
    w = ns * LANES
    nblk = n_pairs // ns
    return pl.pallas_call(
        functools.partial(_sb_kernel, tq=tq, n_streams=ns),
        grid=(b, nblk, s // tq),
        in_specs=[
            pl.BlockSpec((1, tq, w), lambda bi, hb, qi: (bi, qi, hb)),
            pl.BlockSpec((1, s, w), lambda bi, hb, qi: (bi, 0, nblk + hb)),
            pl.BlockSpec((1, s, w), lambda bi, hb, qi: (bi, 0, 2 * nblk + hb)),
            pl.BlockSpec((tq, tq), lambda bi, hb, qi: (0, 0)),
        ],
        out_specs=pl.BlockSpec((1, tq, w), lambda bi, hb, qi: (bi, qi, hb)),
        out_shape=jax.ShapeDtypeStruct((b, s, n_pairs * LANES), BF16),
        scratch_shapes=[
            pltpu.VMEM((ns, 2 * tq, tq), F32),
            pltpu.VMEM((ns, 2 * tq, tq), BF16),
            pltpu.VMEM((ns, 2 * tq, tq), F32),
            pltpu.VMEM((ns, 2 * tq, tq), BF16),
            pltpu.VMEM((ns, 2 * tq, LANES), F32),
            pltpu.VMEM((ns, 2 * tq, LANES), F32),
        ],
        compiler_params=_params("parallel", "parallel", "arbitrary"),
        name="sb_attention",
    )(qkv, qkv, qkv, _later_keys_matrix(tq))


def _fox_gate_kernel(x_ref, g_ref, w_ref, b_ref, tri_ref, c_ref, ct_ref, carry_ref):
    @pl.when(pl.program_id(1) == 0)
    def _():
        carry_ref[...] = jnp.zeros_like(carry_ref)

    tm = x_ref.shape[1]
    u = _rms(x_ref[0], g_ref[...]).astype(BF16)
    logit = jnp.dot(u, w_ref[...], preferred_element_type=F32) + b_ref[...]
    log_f = -_softplus(-logit)
    hi = log_f.astype(BF16)
    r = log_f - hi.astype(F32)
    mid = r.astype(BF16)
    lo = (r - mid.astype(F32)).astype(BF16)
    tri = tri_ref[...]
    c = (jnp.dot(tri, hi, preferred_element_type=F32)
         + jnp.dot(tri, mid, preferred_element_type=F32)
         + jnp.dot(tri, lo, preferred_element_type=F32)) + carry_ref[0:1, :]
    c2 = c * LOG2E
    c_ref[0] = c2
    ct_ref[0] = c2.T
    carry_ref[...] = jnp.broadcast_to(c[tm - 1:tm, :], carry_ref.shape)


def fox_cumulative_log_forget(h, g, w_f, b_f):
    b, s, d = h.shape
    nh = w_f.shape[1]
    tm = GATE_TILE
    w_pad = jnp.zeros((d, LANES), BF16).at[:, :nh].set(w_f)
    b_pad = jnp.zeros((1, LANES), F32).at[0, :nh].set(b_f)
    tri = (jnp.arange(tm)[:, None] >= jnp.arange(tm)[None, :]).astype(BF16)
    return pl.pallas_call(
        _fox_gate_kernel,
        grid=(b, s // tm),
        in_specs=[
            pl.BlockSpec((1, tm, d), lambda bi, si: (bi, si, 0)),
            pl.BlockSpec((1, d), lambda bi, si: (0, 0)),
            pl.BlockSpec((d, LANES), lambda bi, si: (0, 0)),
            pl.BlockSpec((1, LANES), lambda bi, si: (0, 0)),
            pl.BlockSpec((tm, tm), lambda bi, si: (0, 0)),
        ],
        out_specs=[pl.BlockSpec((1, tm, LANES), lambda bi, si: (bi, si, 0)),
                   pl.BlockSpec((1, LANES, tm), lambda bi, si: (bi, 0, si))],
        out_shape=[jax.ShapeDtypeStruct((b, s, LANES), F32), jax.ShapeDtypeStruct((b, LANES, s), F32)],
        scratch_shapes=[pltpu.VMEM((8, LANES), F32)],
        compiler_params=_params("parallel", "arbitrary"),
        name="fox_gate",
    )(h, g.reshape(1, d), w_pad, b_pad, tri)


def _fox_kernel(q_ref, k_ref, v_ref, cq_ref, ck_ref, o_ref, vext_ref, s_ref, w_ref, m_ref, alpha_ref, acc_ref,
                *, tq, n_streams):
    hb = pl.program_id(1)
    qi = pl.program_id(2)
    lanes = [slice(p * LANES, (p + 1) * LANES) for p in range(n_streams)]

    @pl.when(qi == 0)
    def _():
        lane = lax.broadcasted_iota(jnp.int32, (vext_ref.shape[0], LANES), 1)
        one = jnp.ones((vext_ref.shape[0], LANES), BF16)
        for p in range(n_streams):
            v = v_ref[0, :, lanes[p]]
            vext_ref[:, 2 * p * LANES:2 * (p + 1) * LANES] = jnp.concatenate(
                [jnp.where(lane < HEAD_DIM, v, one), jnp.where(lane < HEAD_DIM, one, v)], axis=1)

    qms = [_stack_heads(q_ref[0, :, lanes[p]]) for p in range(n_streams)]
    row_c = lax.broadcasted_iota(jnp.int32, (ROW_CHUNK, tq), 0)
    col_c = lax.broadcasted_iota(jnp.int32, (ROW_CHUNK, tq), 1)

    cq = cq_ref[0]
    head_lane = lax.broadcasted_iota(jnp.int32, cq.shape, 1)
    c_rows = [jnp.concatenate(
        [jnp.broadcast_to(
            jnp.sum(jnp.where(head_lane == 2 * (hb * n_streams + p) + a, cq, 0.0), axis=1, keepdims=True),
            (tq, LANES)) for a in (0, 1)], axis=0) for p in range(n_streams)]

    def scores(p, g):
        start = pl.multiple_of(g * tq, tq)
        z = _scores(qms[p], k_ref[0, pl.ds(start, tq), lanes[p]])
        cols = []
        for c in range(tq // LANES):
            ck = jnp.concatenate(
                [jnp.broadcast_to(ck_ref[0, 2 * p + a:2 * p + a + 1, pl.ds(start + c * LANES, LANES)], (tq, LANES))
                 for a in (0, 1)], axis=0)
            cols.append(z[:, c * LANES:(c + 1) * LANES] + (c_rows[p] - ck))
        return jnp.concatenate(cols, axis=1)

    def value_matmul(p, g):
        start = pl.multiple_of(g * tq, tq)
        halves = []
        for a in (0, 1):
            v_cols = slice((2 * p + a) * LANES, (2 * p + a + 1) * LANES)
            halves.append(jnp.dot(w_ref[p, a * tq:(a + 1) * tq], vext_ref[pl.ds(start, tq), v_cols],
                                  preferred_element_type=F32))
        return jnp.concatenate(halves, axis=0)

    def trip(g, first):
        pvs = []
        for p in range(n_streams):
            for r in range(0, 2 * tq, ROW_CHUNK):
                rows = slice(r, r + ROW_CHUNK)
                s = s_ref[p, rows]
                if first:
                    s = jnp.where(col_c <= row_c + (r % tq), s, -jnp.inf)
                    m_new = jnp.broadcast_to(jnp.max(s, axis=1, keepdims=True), (ROW_CHUNK, LANES))
                else:
                    m_old = m_ref[p, rows]
                    m_new = jnp.maximum(m_old, jnp.broadcast_to(jnp.max(s, axis=1, keepdims=True), (ROW_CHUNK, LANES)))
                    alpha_ref[p, rows] = jnp.exp2(m_old - m_new)
                m_ref[p, rows] = m_new
                w_ref[p, rows] = jnp.concatenate(
                    [jnp.exp2(s[:, c * LANES:(c + 1) * LANES] - m_new).astype(BF16) for c in range(tq // LANES)],
                    axis=1)
            s_ref[p] = scores(p, jnp.maximum(g - 1, 0))
            pvs.append(value_matmul(p, g))
        for p in range(n_streams):
            if first:
                acc_ref[p] = pvs[p]
            else:
                acc_ref[p] = acc_ref[p] * alpha_ref[p] + pvs[p]

    for p in range(n_streams):
        s_ref[p] = scores(p, qi)
    trip(qi, True)

    def body(i, _):
        trip(qi - 1 - i, False)
        return 0

    lax.fori_loop(0, qi, body, 0)
    for p in range(n_streams):
        acc = acc_ref[p]
        o_ref[0, :, lanes[p]] = _unstack_heads(acc / pltpu.roll(acc, HEAD_DIM, 1)).astype(o_ref.dtype)


def fox_attention(qkv, c, c_t, n_pairs):
    b, s, _ = qkv.shape
    tq, ns = FOX_TQ, FOX_STREAMS
    w = ns * LANES
    nblk = n_pairs // ns
    return pl.pallas_call(
        functools.partial(_fox_kernel, tq=tq, n_streams=ns),
        grid=(b, nblk, s // tq),
        in_specs=[
            pl.BlockSpec((1, tq, w), lambda bi, hb, qi: (bi, qi, hb)),
            pl.BlockSpec((1, s, w), lambda bi, hb, qi: (bi, 0, nblk + hb)),
            pl.BlockSpec((1, s, w), lambda bi, hb, qi: (bi, 0, 2 * nblk + hb)),
            pl.BlockSpec((1, tq, LANES), lambda bi, hb, qi: (bi, qi, 0)),
            pl.BlockSpec((1, 2 * ns, s), lambda bi, hb, qi: (bi, hb, 0)),
        ],
        out_specs=pl.BlockSpec((1, tq, w), lambda bi, hb, qi: (bi, qi, hb)),
        out_shape=jax.ShapeDtypeStruct((b, s, n_pairs * LANES), BF16),
        scratch_shapes=[
            pltpu.VMEM((s, ns * 2 * LANES), BF16),
            pltpu.VMEM((ns, 2 * tq, tq), F32),
            pltpu.VMEM((ns, 2 * tq, tq), BF16),
            pltpu.VMEM((ns, 2 * tq, LANES), F32),
            pltpu.VMEM((ns, 2 * tq, LANES), F32),
            pltpu.VMEM((ns, 2 * tq, LANES), F32),
        ],
        compiler_params=_params("parallel", "parallel", "arbitrary"),
        name="fox_attention",
    )(qkv, qkv, qkv, c, c_t)


def _swa_kernel(q_ref, kp_ref, kc_ref, vp_ref, vc_ref, bias_ref, sink_ref, o_ref, *, blk, n_blocks):
    qi = pl.program_id(2)
    n_pairs_here = q_ref.shape[2] // LANES
    k_all = jnp.concatenate([kp_ref[0], kc_ref[0]], axis=0)
    v_all = jnp.concatenate([vp_ref[0], vc_ref[0]], axis=0)
    lane = lax.broadcasted_iota(jnp.int32, v_all.shape, 1)
    one = jnp.ones(v_all.shape, BF16)
    v_heads = [jnp.where(lane < HEAD_DIM, v_all, one), jnp.where(lane < HEAD_DIM, one, v_all)]
    col = lax.broadcasted_iota(jnp.int32, (2 * blk, 2 * blk), 1)
    chains = [(r, a) for r in range(n_blocks) for a in range(n_pairs_here)]
    logits = []
    for r, a in chains:
        qm = _stack_heads(q_ref[0, r * blk:(r + 1) * blk, a * LANES:(a + 1) * LANES])
        s = _scores(qm, k_all[r * blk:(r + 2) * blk]) + bias_ref[2 * a:2 * a + 2].reshape(2 * blk, 2 * blk)
        if r == 0:
            s = jnp.where((col >= blk) | (qi > 0), s, -jnp.inf)
        logits.append(s)
    sums = []
    for (r, a), s in zip(chains, logits):
        sink = sink_ref[a]
        m = jnp.maximum(jnp.broadcast_to(jnp.max(s, axis=1, keepdims=True), (2 * blk, LANES)), sink)
        p = jnp.concatenate(
            [jnp.exp2(s[:, c * LANES:(c + 1) * LANES] - m).astype(BF16) for c in range(2 * blk // LANES)], axis=1)
        pv = jnp.concatenate(
            [jnp.dot(p[h * blk:(h + 1) * blk], v_heads[h][r * blk:(r + 2) * blk], preferred_element_type=F32)
             for h in (0, 1)], axis=0)
        sums.append((pv, jnp.exp2(sink - m)))
    for (r, a), (pv, sink_term) in zip(chains, sums):
        o = pv / (pltpu.roll(pv, HEAD_DIM, 1) + sink_term)
        o_ref[0, r * blk:(r + 1) * blk, a * LANES:(a + 1) * LANES] = _unstack_heads(o).astype(o_ref.dtype)


def _t5_bucket(dist):
    max_exact = REL_BUCKETS // 2
    d = jnp.maximum(dist, 1).astype(F32)
    large = max_exact + (jnp.log(d / max_exact) / math.log(REL_MAX_DIST / max_exact)
                         * (REL_BUCKETS - max_exact)).astype(jnp.int32)
    large = jnp.minimum(large, REL_BUCKETS - 1)
    return jnp.where(dist < max_exact, dist, large)


def swa_attention(proj, sinks, rel_bias, n_pairs):
    b, s, _ = proj.shape
    blk, nb = SWA_BLOCK, SWA_BLOCKS_PER_STEP
    tq = blk * nb
    group_pairs = n_pairs // SWA_KV_HEADS
    gw = group_pairs * LANES
    in_window = rel_bias.astype(F32)[_t5_bucket(jnp.arange(WINDOW))].T * LOG2E
    pad = jnp.full((in_window.shape[0], blk - 1), -jnp.inf, F32)
    by_key = jnp.concatenate(
        [jnp.full((in_window.shape[0], 2 * blk - WINDOW), -jnp.inf, F32), in_window[:, ::-1], pad], axis=1)
    bias = jnp.stack([by_key[:, blk - 1 - q:3 * blk - 1 - q] for q in range(blk)], axis=1)
    sink_rows = jnp.broadcast_to((sinks.astype(F32) * LOG2E).reshape(n_pairs, 2, 1, 1),
                                 (n_pairs, 2, blk, LANES)).reshape(n_pairs, 2 * blk, LANES)
    return pl.pallas_call(
        functools.partial(_swa_kernel, blk=blk, n_blocks=nb),
        grid=(b, SWA_KV_HEADS, s // tq),
        in_specs=[
            pl.BlockSpec((1, tq, gw), lambda bi, gk, qi: (bi, qi, gk)),
            pl.BlockSpec((1, blk, LANES), lambda bi, gk, qi: (bi, jnp.maximum(qi * nb - 1, 0), n_pairs + gk)),
            pl.BlockSpec((1, tq, LANES), lambda bi, gk, qi: (bi, qi, n_pairs + gk)),
            pl.BlockSpec((1, blk, LANES),
                         lambda bi, gk, qi: (bi, jnp.maximum(qi * nb - 1, 0), n_pairs + SWA_KV_HEADS + gk)),
            pl.BlockSpec((1, tq, LANES), lambda bi, gk, qi: (bi, qi, n_pairs + SWA_KV_HEADS + gk)),
            pl.BlockSpec((2 * group_pairs, blk, 2 * blk), lambda bi, gk, qi: (gk, 0, 0)),
            pl.BlockSpec((group_pairs, 2 * blk, LANES), lambda bi, gk, qi: (gk, 0, 0)),
        ],
        out_specs=pl.BlockSpec((1, tq, gw), lambda bi, gk, qi: (bi, qi, gk)),
        out_shape=jax.ShapeDtypeStruct((b, s, n_pairs * LANES), BF16),
        compiler_params=_params("parallel", "parallel", "arbitrary"),
        name="swa_attention",
    )(proj, proj, proj, proj, proj, bias, sink_rows)


def _duplicate_heads(w, n_heads):
    d = w.shape[0]
    w = w.reshape(d, n_heads, HEAD_DIM)
    return jnp.concatenate([w, w], axis=2).reshape(d, n_heads * LANES)


def _scale_q_columns(w_in, d):
    return jnp.concatenate([w_in[..., :d] * (SCALE * LOG2E), w_in[..., d:]], axis=-1).astype(BF16)


def _swa_projection_weights(w_in, d):
    kw = SWA_KV_HEADS * HEAD_DIM
    w = _scale_q_columns(w_in, d)
    return jnp.concatenate(
        [w[:, :d], _duplicate_heads(w[:, d:d + kw], SWA_KV_HEADS), _duplicate_heads(w[:, d + kw:], SWA_KV_HEADS)],
        axis=1)


def kernel(x, p, attn_norm, mlp_norm, ple_norm, final_norm, w_in_sb, w_out_sb, w_in_fox, b_forget, w_out_fox,
           w_in_swa, sinks, w_out_swa, rel_bias, w_up, w_down, w_ple, w_ple_gate):
    b, s, d = x.shape
    depth = p.shape[0]
    t = b * s
    n_pairs = d // LANES
    h = x.reshape(t, d)
    w_in_sb, w_in_fox = _scale_q_columns(w_in_sb, d), _scale_q_columns(w_in_fox, d)
    w_out = {0: w_out_sb.astype(BF16), 1: w_out_fox.astype(BF16), 2: w_out_swa.astype(BF16)}
    w_up, w_down, w_ple, w_ple_gate = (w.astype(BF16) for w in (w_up, w_down, w_ple, w_ple_gate))
    for i in range(depth):
        kind = i % N_MIXERS
        j = i // N_MIXERS
        if kind == 0:
            qkv = norm_matmul(h, attn_norm[i], w_in_sb[j])
            o = sb_attention(qkv.reshape(b, s, 3 * d), n_pairs)
        elif kind == 1:
            qkv = norm_matmul(h, attn_norm[i], w_in_fox[j][:, :3 * d])
            c, c_t = fox_cumulative_log_forget(h.reshape(b, s, d), attn_norm[i], w_in_fox[j][:, 3 * d:], b_forget[j])
            o = fox_attention(qkv.reshape(b, s, 3 * d), c, c_t, n_pairs)
        else:
            proj = norm_matmul(h, attn_norm[i], _swa_projection_weights(w_in_swa[j], d))
            o = swa_attention(proj.reshape(b, s, 2 * d), sinks[j], rel_bias, n_pairs)
        h = post_attention(o.reshape(t, d), h, p[i].reshape(t, p.shape[-1]), w_out[kind][j], mlp_norm[i], w_up[i],
                           w_down[i], ple_norm[i], w_ple_gate[i], w_ple[i], final_norm, final=(i == depth - 1))
    return h.reshape(b, s, d)
```

```python
import functools
import math

import jax
import jax.numpy as jnp
from jax import lax
from jax.experimental import pallas as pl
from jax.experimental.pallas import tpu as pltpu

F32 = jnp.float32
BF16 = jnp.bfloat16

HEAD_DIM = 64
LANES = 128
N_MIXERS = 3
SWA_KV_HEADS = 4
WINDOW = 128
REL_BUCKETS = 32
REL_MAX_DIST = 128
EPS = 1e-6
SCALE = HEAD_DIM ** -0.5
LOG2E = math.log2(math.e)
SIGN_BIT = -2 ** 31
UNDERFLOW_LOG2 = -152.0

VMEM_LIMIT = 48 * 1024 * 1024
TOKEN_TILE = 512
FF_TILE = 2048
SB_TQ = 256
ROW_CHUNK = 32
SB_STREAMS = 4
FOX_TQ = 256
FOX_STREAMS = 4
SWA_BLOCK = 128
SWA_BLOCKS_PER_STEP = 8
GATE_TILE = 512


def _params(*sem):
    return pltpu.CompilerParams(dimension_semantics=sem, vmem_limit_bytes=VMEM_LIMIT)


def _rms(x, g):
    return x * lax.rsqrt(jnp.mean(x * x, axis=-1, keepdims=True) + EPS) * g


def _softplus2(z2):
    neg_abs = lax.bitcast_convert_type(lax.bitcast_convert_type(z2, jnp.int32) | jnp.int32(SIGN_BIT), F32)
    return jnp.maximum(z2, 0.0) + LOG2E * jnp.log(1.0 + jnp.exp2(neg_abs))


def _softplus(z):
    return jnp.maximum(z, 0.0) + jnp.log(1.0 + jnp.exp(-jnp.abs(z)))


def _norm_matmul_kernel(x_ref, g_ref, w_ref, o_ref):
    u = _rms(x_ref[...], g_ref[...]).astype(BF16)
    o_ref[...] = jnp.dot(u, w_ref[...], preferred_element_type=F32).astype(o_ref.dtype)


def norm_matmul(x, g, w, layer, n):
    t, d = x.shape
    tm = TOKEN_TILE
    return pl.pallas_call(
        _norm_matmul_kernel,
        grid=(t // tm,),
        in_specs=[
            pl.BlockSpec((tm, d), lambda i: (i, 0)),
            pl.BlockSpec((1, d), lambda i: (0, 0)),
            pl.BlockSpec((None, d, n), lambda i: (layer, 0, 0)),
        ],
        out_specs=pl.BlockSpec((tm, n), lambda i: (i, 0)),
        out_shape=jax.ShapeDtypeStruct((t, n), BF16),
        compiler_params=_params("parallel"),
        name="norm_matmul",
    )(x, g.reshape(1, d), w)


def _post_attention_kernel(o_ref, h_ref, p_ref, wo_ref, gm_ref, wu_ref, wd_ref, gp_ref, wg_ref, wp_ref, fn_ref,
                           out_ref, u_ref, *, final):
    f = pl.program_id(1)

    @pl.when(f == 0)
    def _():
        h = h_ref[...] + jnp.dot(o_ref[...], wo_ref[...], preferred_element_type=F32)
        u_ref[...] = _rms(h, gm_ref[...]).astype(BF16)
        out_ref[...] = h

    a = jnp.maximum(jnp.dot(u_ref[...], wu_ref[...], preferred_element_type=F32), 0.0)
    out_ref[...] += jnp.dot((a * a).astype(BF16), wd_ref[...], preferred_element_type=F32)

    @pl.when(f == pl.num_programs(1) - 1)
    def _():
        h = out_ref[...]
        u = _rms(h, gp_ref[...]).astype(BF16)
        gate = 1.0 / (1.0 + jnp.exp(-jnp.dot(u, wg_ref[...], preferred_element_type=F32)))
        e = jnp.dot(p_ref[...].astype(BF16), wp_ref[...], preferred_element_type=F32)
        hn = h + e * gate
        if final:
            hn = _rms(hn, fn_ref[...])
        out_ref[...] = hn


def post_attention(o, h, p, w_out, g_mlp, w_up, w_down, g_ple, w_gate, w_ple, final_g, layer, out_layer, final):
    t, d = h.shape
    f = w_up.shape[2]
    pd = p.shape[2]
    tm, tf = TOKEN_TILE, FF_TILE
    row = lambda i, j: (i, 0)
    fixed = lambda i, j: (0, 0)
    return pl.pallas_call(
        functools.partial(_post_attention_kernel, final=final),
        grid=(t // tm, f // tf),
        in_specs=[
            pl.BlockSpec((tm, d), row),
            pl.BlockSpec((tm, d), row),
            pl.BlockSpec((None, tm, pd), lambda i, j: (layer, i, 0)),
            pl.BlockSpec((None, d, d), lambda i, j: (out_layer, 0, 0)),
            pl.BlockSpec((1, d), fixed),
            pl.BlockSpec((None, d, tf), lambda i, j: (layer, 0, j)),
            pl.BlockSpec((None, tf, d), lambda i, j: (layer, j, 0)),
            pl.BlockSpec((1, d), fixed),
            pl.BlockSpec((None, d, d), lambda i, j: (layer, 0, 0)),
            pl.BlockSpec((None, pd, d), lambda i, j: (layer, 0, 0)),
            pl.BlockSpec((1, d), fixed),
        ],
        out_specs=pl.BlockSpec((tm, d), row),
        out_shape=jax.ShapeDtypeStruct((t, d), F32),
        scratch_shapes=[pltpu.VMEM((tm, d), BF16)],
        compiler_params=_params("parallel", "arbitrary"),
        name="post_attention",
    )(o, h, p, w_out, g_mlp.reshape(1, d), w_up, w_down, g_ple.reshape(1, d), w_gate, w_ple, final_g.reshape(1, d))


def _stack_heads(q):
    tq = q.shape[0]
    lane = lax.broadcasted_iota(jnp.int32, (tq, LANES), 1)
    zero = jnp.zeros_like(q)
    return jnp.concatenate(
        [jnp.where(lane < HEAD_DIM, q, zero), jnp.where(lane >= HEAD_DIM, q, zero)], axis=0)


def _unstack_heads(acc):
    tq = acc.shape[0] // 2
    lane = lax.broadcasted_iota(jnp.int32, (tq, LANES), 1)
    return jnp.where(lane < HEAD_DIM, acc[:tq], acc[tq:])


def _scores(qm, kb):
    return lax.dot_general(qm, kb, (((1,), (1,)), ((), ())), preferred_element_type=F32)


def _sb_kernel(q_ref, k_ref, v_ref, u_ref, o_ref, z_ref, hi_ref, zms_ref, w_ref, carry_ref, acc_ref,
               *, tq, n_streams):
    qi = pl.program_id(2)
    lanes = [slice(p * LANES, (p + 1) * LANES) for p in range(n_streams)]
    qms = [_stack_heads(q_ref[0, :, lanes[p]]) for p in range(n_streams)]
    row_c = lax.broadcasted_iota(jnp.int32, (ROW_CHUNK, tq), 0)
    col_c = lax.broadcasted_iota(jnp.int32, (ROW_CHUNK, tq), 1)
    lane_c = lax.broadcasted_iota(jnp.int32, (ROW_CHUNK, LANES), 1)
    umat = u_ref[...]

    def scores(p, g):
        start = pl.multiple_of(g * tq, tq)
        return _scores(qms[p], k_ref[0, pl.ds(start, tq), lanes[p]])

    def trip(g, first):
        masked = first
        v_start = pl.multiple_of(g * tq, tq)
        cts = []
        for p in range(n_streams):
            for r in range(0, 2 * tq, ROW_CHUNK):
                z = z_ref[p, r:r + ROW_CHUNK]
                sp = _softplus2(z)
                if masked:
                    sp = jnp.where(col_c < row_c + (r % tq), sp, 0.0)
                hi_ref[p, r:r + ROW_CHUNK] = sp.astype(BF16)
                zms_ref[p, r:r + ROW_CHUNK] = z - sp
            cts.append(jnp.dot(hi_ref[p], umat, preferred_element_type=F32))
            z_ref[p] = scores(p, jnp.maximum(g - 1, 0))
        pvs = []
        top = None
        for p in range(n_streams):
            for r in range(0, 2 * tq, ROW_CHUNK):
                ct = cts[p][r:r + ROW_CHUNK]
                ct_last = ct[:, tq - LANES:]
                ct_keys = jnp.concatenate(
                    [ct[:, :tq - LANES], jnp.where(lane_c == LANES - 1, 0.0, ct_last)], axis=1)
                e = zms_ref[p, r:r + ROW_CHUNK] + ct_keys
                if masked:
                    e = jnp.where(col_c < row_c + (r % tq), e, -jnp.inf)
                total = jnp.broadcast_to(ct_last[:, LANES - 1:], (ROW_CHUNK, LANES))
                if first:
                    cols = [e[:, c * LANES:(c + 1) * LANES] for c in range(tq // LANES)]
                    carry_new = total
                else:
                    carry = carry_ref[p, r:r + ROW_CHUNK]
                    cols = [e[:, c * LANES:(c + 1) * LANES] + carry for c in range(tq // LANES)]
                    carry_new = carry + total
                carry_ref[p, r:r + ROW_CHUNK] = carry_new
                top = carry_new if top is None else jnp.maximum(top, carry_new)
                w_ref[p, r:r + ROW_CHUNK] = jnp.concatenate([jnp.exp2(c).astype(BF16) for c in cols], axis=1)
            pvs.append(jnp.dot(w_ref[p], v_ref[0, pl.ds(v_start, tq), lanes[p]], preferred_element_type=F32))
        for p in range(n_streams):
            if first:
                acc_ref[p] = pvs[p]
            else:
                acc_ref[p] += pvs[p]
        return jnp.max(top)

    for p in range(n_streams):
        z_ref[p] = scores(p, qi)
    top0 = trip(qi, True)

    def keep_going(state):
        g, top = state
        return (g >= 0) & (top > UNDERFLOW_LOG2)

    def body(state):
        g, _ = state
        return g - 1, trip(g, False)

    lax.while_loop(keep_going, body, (qi - 1, top0))
    for p in range(n_streams):
        o_ref[0, :, lanes[p]] = _unstack_heads(acc_ref[p]).astype(o_ref.dtype)


def _later_keys_matrix(tq):
    j = jnp.arange(tq)[:, None]
    s = jnp.arange(tq)[None, :]
    return -((j > s) | (s == tq - 1)).astype(BF16)


def sb_attention(qkv, n_pairs):
    b, s, _ = qkv.shape
    tq, ns = SB_TQ, SB_STREAMS
    w = ns * LANES
    nblk = n_pairs // ns
    return pl.pallas_call(
        functools.partial(_sb_kernel, tq=tq, n_streams=ns),
        grid=(b, nblk, s // tq),
        in_specs=[
            pl.BlockSpec((1, tq, w), lambda bi, hb, qi: (bi, qi, hb)),
            pl.BlockSpec((1, s, w), lambda bi, hb, qi: (bi, 0, nblk + hb)),
            pl.BlockSpec((1, s, w), lambda bi, hb, qi: (bi, 0, 2 * nblk + hb)),
            pl.BlockSpec((tq, tq), lambda bi, hb, qi: (0, 0)),
        ],
        out_specs=pl.BlockSpec((1, tq, w), lambda bi, hb, qi: (bi, qi, hb)),
        out_shape=jax.ShapeDtypeStruct((b, s, n_pairs * LANES), BF16),
        scratch_shapes=[
            pltpu.VMEM((ns, 2 * tq, tq), F32),
            pltpu.VMEM((ns, 2 * tq, tq), BF16),
            pltpu.VMEM((ns, 2 * tq, tq), F32),
            pltpu.VMEM((ns, 2 * tq, tq), BF16),
            pltpu.VMEM((ns, 2 * tq, LANES), F32),
            pltpu.VMEM((ns, 2 * tq, LANES), F32),
        ],
        compiler_params=_params("parallel", "parallel", "arbitrary"),
        name="sb_attention",
    )(qkv, qkv, qkv, _later_keys_matrix(tq))


def _fox_gate_kernel(x_ref, g_ref, w_ref, b_ref, tri_ref, c_ref, ct_ref, carry_ref):
    @pl.when(pl.program_id(1) == 0)
    def _():
        carry_ref[...] = jnp.zeros_like(carry_ref)

    tm = x_ref.shape[1]
    u = _rms(x_ref[0], g_ref[...]).astype(BF16)
    logit = jnp.dot(u, w_ref[...], preferred_element_type=F32) + b_ref[...]
    log_f = -_softplus(-logit)
    hi = log_f.astype(BF16)
    r = log_f - hi.astype(F32)
    mid = r.astype(BF16)
    lo = (r - mid.astype(F32)).astype(BF16)
    tri = tri_ref[...]
    c = (jnp.dot(tri, hi, preferred_element_type=F32)
         + jnp.dot(tri, mid, preferred_element_type=F32)
         + jnp.dot(tri, lo, preferred_element_type=F32)) + carry_ref[0:1, :]
    c2 = c * LOG2E
    c_ref[0] = c2
    ct_ref[0] = c2.T
    carry_ref[...] = jnp.broadcast_to(c[tm - 1:tm, :], carry_ref.shape)


def fox_cumulative_log_forget(h, g, w_f, b_f):
    b, s, d = h.shape
    nh = w_f.shape[1]
    tm = GATE_TILE
    w_pad = jnp.zeros((d, LANES), BF16).at[:, :nh].set(w_f)
    b_pad = jnp.zeros((1, LANES), F32).at[0, :nh].set(b_f)
    tri = (jnp.arange(tm)[:, None] >= jnp.arange(tm)[None, :]).astype(BF16)
    return pl.pallas_call(
        _fox_gate_kernel,
        grid=(b, s // tm),
        in_specs=[
            pl.BlockSpec((1, tm, d), lambda bi, si: (bi, si, 0)),
            pl.BlockSpec((1, d), lambda bi, si: (0, 0)),
            pl.BlockSpec((d, LANES), lambda bi, si: (0, 0)),
            pl.BlockSpec((1, LANES), lambda bi, si: (0, 0)),
            pl.BlockSpec((tm, tm), lambda bi, si: (0, 0)),
        ],
        out_specs=[pl.BlockSpec((1, tm, LANES), lambda bi, si: (bi, si, 0)),
                   pl.BlockSpec((1, LANES, tm), lambda bi, si: (bi, 0, si))],
        out_shape=[jax.ShapeDtypeStruct((b, s, LANES), F32), jax.ShapeDtypeStruct((b, LANES, s), F32)],
        scratch_shapes=[pltpu.VMEM((8, LANES), F32)],
        compiler_params=_params("parallel", "arbitrary"),
        name="fox_gate",
    )(h, g.reshape(1, d), w_pad, b_pad, tri)


def _fox_kernel(q_ref, k_ref, v_ref, cq_ref, ck_ref, o_ref, vext_ref, s_ref, w_ref, m_ref, alpha_ref, acc_ref,
                *, tq, n_streams):
    hb = pl.program_id(1)
    qi = pl.program_id(2)
    lanes = [slice(p * LANES, (p + 1) * LANES) for p in range(n_streams)]

    @pl.when(qi == 0)
    def _():
        lane = lax.broadcasted_iota(jnp.int32, (vext_ref.shape[0], LANES), 1)
        one = jnp.ones((vext_ref.shape[0], LANES), BF16)
        for p in range(n_streams):
            v = v_ref[0, :, lanes[p]]
            vext_ref[:, 2 * p * LANES:2 * (p + 1) * LANES] = jnp.concatenate(
                [jnp.where(lane < HEAD_DIM, v, one), jnp.where(lane < HEAD_DIM, one, v)], axis=1)

    qms = [_stack_heads(q_ref[0, :, lanes[p]]) for p in range(n_streams)]
    row_c = lax.broadcasted_iota(jnp.int32, (ROW_CHUNK, tq), 0)
    col_c = lax.broadcasted_iota(jnp.int32, (ROW_CHUNK, tq), 1)

    cq = cq_ref[0]
    head_lane = lax.broadcasted_iota(jnp.int32, cq.shape, 1)
    c_rows = [jnp.concatenate(
        [jnp.broadcast_to(
            jnp.sum(jnp.where(head_lane == 2 * (hb * n_streams + p) + a, cq, 0.0), axis=1, keepdims=True),
            (tq, LANES)) for a in (0, 1)], axis=0) for p in range(n_streams)]

    def scores(p, g):
        start = pl.multiple_of(g * tq, tq)
        z = _scores(qms[p], k_ref[0, pl.ds(start, tq), lanes[p]])
        cols = []
        for c in range(tq // LANES):
            ck = jnp.concatenate(
                [jnp.broadcast_to(ck_ref[0, 2 * p + a:2 * p + a + 1, pl.ds(start + c * LANES, LANES)], (tq, LANES))
                 for a in (0, 1)], axis=0)
            cols.append(z[:, c * LANES:(c + 1) * LANES] + (c_rows[p] - ck))
        return jnp.concatenate(cols, axis=1)

    def value_matmul(p, g):
        start = pl.multiple_of(g * tq, tq)
        halves = []
        for a in (0, 1):
            v_cols = slice((2 * p + a) * LANES, (2 * p + a + 1) * LANES)
            halves.append(jnp.dot(w_ref[p, a * tq:(a + 1) * tq], vext_ref[pl.ds(start, tq), v_cols],
                                  preferred_element_type=F32))
        return jnp.concatenate(halves, axis=0)

    def trip(g, first):
        if not first:
            pvs = [value_matmul(p, g + 1) for p in range(n_streams)]
        for p in range(n_streams):
            for r in range(0, 2 * tq, ROW_CHUNK):
                rows = slice(r, r + ROW_CHUNK)
                s = s_ref[p, rows]
                if first:
                    s = jnp.where(col_c <= row_c + (r % tq), s, -jnp.inf)
                    m_new = jnp.broadcast_to(jnp.max(s, axis=1, keepdims=True), (ROW_CHUNK, LANES))
                    acc_ref[p, rows] = jnp.zeros((ROW_CHUNK, LANES), F32)
                    alpha_ref[p, rows] = jnp.zeros((ROW_CHUNK, LANES), F32)
                else:
                    acc_ref[p, rows] = acc_ref[p, rows] * alpha_ref[p, rows] + pvs[p][rows]
                    m_old = m_ref[p, rows]
                    m_new = jnp.maximum(m_old, jnp.broadcast_to(jnp.max(s, axis=1, keepdims=True), (ROW_CHUNK, LANES)))
                    alpha_ref[p, rows] = jnp.exp2(m_old - m_new)
                m_ref[p, rows] = m_new
                w_ref[p, rows] = jnp.concatenate(
                    [jnp.exp2(s[:, c * LANES:(c + 1) * LANES] - m_new).astype(BF16) for c in range(tq // LANES)],
                    axis=1)
            s_ref[p] = scores(p, jnp.maximum(g - 1, 0))

    for p in range(n_streams):
        s_ref[p] = scores(p, qi)
    trip(qi, True)

    def body(i, _):
        trip(qi - 1 - i, False)
        return 0

    lax.fori_loop(0, qi, body, 0)
    for p in range(n_streams):
        acc = acc_ref[p] * alpha_ref[p] + value_matmul(p, 0)
        o_ref[0, :, lanes[p]] = _unstack_heads(acc / pltpu.roll(acc, HEAD_DIM, 1)).astype(o_ref.dtype)


def fox_attention(qkv, c, c_t, n_pairs):
    b, s, _ = qkv.shape
    tq, ns = FOX_TQ, FOX_STREAMS
    w = ns * LANES
    nblk = n_pairs // ns
    return pl.pallas_call(
        functools.partial(_fox_kernel, tq=tq, n_streams=ns),
        grid=(b, nblk, s // tq),
        in_specs=[
            pl.BlockSpec((1, tq, w), lambda bi, hb, qi: (bi, qi, hb)),
            pl.BlockSpec((1, s, w), lambda bi, hb, qi: (bi, 0, nblk + hb)),
            pl.BlockSpec((1, s, w), lambda bi, hb, qi: (bi, 0, 2 * nblk + hb)),
            pl.BlockSpec((1, tq, LANES), lambda bi, hb, qi: (bi, qi, 0)),
            pl.BlockSpec((1, 2 * ns, s), lambda bi, hb, qi: (bi, hb, 0)),
        ],
        out_specs=pl.BlockSpec((1, tq, w), lambda bi, hb, qi: (bi, qi, hb)),
        out_shape=jax.ShapeDtypeStruct((b, s, n_pairs * LANES), BF16),
        scratch_shapes=[
            pltpu.VMEM((s, ns * 2 * LANES), BF16),
            pltpu.VMEM((ns, 2 * tq, tq), F32),
            pltpu.VMEM((ns, 2 * tq, tq), BF16),
            pltpu.VMEM((ns, 2 * tq, LANES), F32),
            pltpu.VMEM((ns, 2 * tq, LANES), F32),
            pltpu.VMEM((ns, 2 * tq, LANES), F32),
        ],
        compiler_params=_params("parallel", "parallel", "arbitrary"),
        name="fox_attention",
    )(qkv, qkv, qkv, c, c_t)


def _swa_kernel(q_ref, kp_ref, kc_ref, vp_ref, vc_ref, bias_ref, sink_ref, o_ref, *, blk, n_blocks):
    qi = pl.program_id(2)
    n_pairs_here = q_ref.shape[2] // LANES
    k_all = jnp.concatenate([kp_ref[0], kc_ref[0]], axis=0)
    v_all = jnp.concatenate([vp_ref[0], vc_ref[0]], axis=0)
    lane = lax.broadcasted_iota(jnp.int32, v_all.shape, 1)
    one = jnp.ones(v_all.shape, BF16)
    v_heads = [jnp.where(lane < HEAD_DIM, v_all, one), jnp.where(lane < HEAD_DIM, one, v_all)]
    col = lax.broadcasted_iota(jnp.int32, (2 * blk, 2 * blk), 1)
    chains = [(r, a) for r in range(n_blocks) for a in range(n_pairs_here)]
    logits = []
    for r, a in chains:
        qm = _stack_heads(q_ref[0, r * blk:(r + 1) * blk, a * LANES:(a + 1) * LANES])
        s = _scores(qm, k_all[r * blk:(r + 2) * blk]) + bias_ref[2 * a:2 * a + 2].reshape(2 * blk, 2 * blk)
        if r == 0:
            s = jnp.where((col >= blk) | (qi > 0), s, -jnp.inf)
        logits.append(s)
    sums = []
    for (r, a), s in zip(chains, logits):
        sink = sink_ref[a]
        m = jnp.maximum(jnp.broadcast_to(jnp.max(s, axis=1, keepdims=True), (2 * blk, LANES)), sink)
        p = jnp.concatenate(
            [jnp.exp2(s[:, c * LANES:(c + 1) * LANES] - m).astype(BF16) for c in range(2 * blk // LANES)], axis=1)
        pv = jnp.concatenate(
            [jnp.dot(p[h * blk:(h + 1) * blk], v_heads[h][r * blk:(r + 2) * blk], preferred_element_type=F32)
             for h in (0, 1)], axis=0)
        sums.append((pv, jnp.exp2(sink - m)))
    for (r, a), (pv, sink_term) in zip(chains, sums):
        o = pv / (pltpu.roll(pv, HEAD_DIM, 1) + sink_term)
        o_ref[0, r * blk:(r + 1) * blk, a * LANES:(a + 1) * LANES] = _unstack_heads(o).astype(o_ref.dtype)


def _t5_bucket(dist):
    max_exact = REL_BUCKETS // 2
    d = jnp.maximum(dist, 1).astype(F32)
    large = max_exact + (jnp.log(d / max_exact) / math.log(REL_MAX_DIST / max_exact)
                         * (REL_BUCKETS - max_exact)).astype(jnp.int32)
    large = jnp.minimum(large, REL_BUCKETS - 1)
    return jnp.where(dist < max_exact, dist, large)


def swa_attention(proj, sinks, rel_bias, n_pairs):
    b, s, _ = proj.shape
    blk, nb = SWA_BLOCK, SWA_BLOCKS_PER_STEP
    tq = blk * nb
    group_pairs = n_pairs // SWA_KV_HEADS
    gw = group_pairs * LANES
    in_window = rel_bias.astype(F32)[_t5_bucket(jnp.arange(WINDOW))].T * LOG2E
    pad = jnp.full((in_window.shape[0], blk - 1), -jnp.inf, F32)
    by_key = jnp.concatenate(
        [jnp.full((in_window.shape[0], 2 * blk - WINDOW), -jnp.inf, F32), in_window[:, ::-1], pad], axis=1)
    bias = jnp.stack([by_key[:, blk - 1 - q:3 * blk - 1 - q] for q in range(blk)], axis=1)
    sink_rows = jnp.broadcast_to((sinks.astype(F32) * LOG2E).reshape(n_pairs, 2, 1, 1),
                                 (n_pairs, 2, blk, LANES)).reshape(n_pairs, 2 * blk, LANES)
    return pl.pallas_call(
        functools.partial(_swa_kernel, blk=blk, n_blocks=nb),
        grid=(b, SWA_KV_HEADS, s // tq),
        in_specs=[
            pl.BlockSpec((1, tq, gw), lambda bi, gk, qi: (bi, qi, gk)),
            pl.BlockSpec((1, blk, LANES), lambda bi, gk, qi: (bi, jnp.maximum(qi * nb - 1, 0), n_pairs + gk)),
            pl.BlockSpec((1, tq, LANES), lambda bi, gk, qi: (bi, qi, n_pairs + gk)),
            pl.BlockSpec((1, blk, LANES),
                         lambda bi, gk, qi: (bi, jnp.maximum(qi * nb - 1, 0), n_pairs + SWA_KV_HEADS + gk)),
            pl.BlockSpec((1, tq, LANES), lambda bi, gk, qi: (bi, qi, n_pairs + SWA_KV_HEADS + gk)),
            pl.BlockSpec((2 * group_pairs, blk, 2 * blk), lambda bi, gk, qi: (gk, 0, 0)),
            pl.BlockSpec((group_pairs, 2 * blk, LANES), lambda bi, gk, qi: (gk, 0, 0)),
        ],
        out_specs=pl.BlockSpec((1, tq, gw), lambda bi, gk, qi: (bi, qi, gk)),
        out_shape=jax.ShapeDtypeStruct((b, s, n_pairs * LANES), BF16),
        compiler_params=_params("parallel", "parallel", "arbitrary"),
        name="swa_attention",
    )(proj, proj, proj, proj, proj, bias, sink_rows)


def _duplicate_heads(w, n_heads):
    d = w.shape[0]
    w = w.reshape(d, n_heads, HEAD_DIM)
    return jnp.concatenate([w, w], axis=2).reshape(d, n_heads * LANES)


def _scale_q_columns(w_in, d):
    return jnp.concatenate([w_in[..., :d] * (SCALE * LOG2E), w_in[..., d:]], axis=-1).astype(BF16)


def _swa_projection_weights(w_in, d):
    kw = SWA_KV_HEADS * HEAD_DIM
    w = _scale_q_columns(w_in, d)
    return jnp.concatenate(
        [w[:, :d], _duplicate_heads(w[:, d:d + kw], SWA_KV_HEADS), _duplicate_heads(w[:, d + kw:], SWA_KV_HEADS)],
        axis=1)


def kernel(x, p, attn_norm, mlp_norm, ple_norm, final_norm, w_in_sb, w_out_sb, w_in_fox, b_forget, w_out_fox,
           w_in_swa, sinks, w_out_swa, rel_bias, w_up, w_down, w_ple, w_ple_gate):
    b, s, d = x.shape
    depth = p.shape[0]
    t = b * s
    n_pairs = d // LANES
    h = x.reshape(t, d)
    p = p.reshape(depth, t, p.shape[-1])
    w_in_sb, w_in_fox = _scale_q_columns(w_in_sb, d), _scale_q_columns(w_in_fox, d)
    w_in_swa = jnp.stack([_swa_projection_weights(w, d) for w in w_in_swa])
    w_out = {0: w_out_sb.astype(BF16), 1: w_out_fox.astype(BF16), 2: w_out_swa.astype(BF16)}
    w_up, w_down, w_ple, w_ple_gate = (w.astype(BF16) for w in (w_up, w_down, w_ple, w_ple_gate))
    for i in range(depth):
        kind = i % N_MIXERS
        j = i // N_MIXERS
        if kind == 0:
            qkv = norm_matmul(h, attn_norm[i], w_in_sb, j, 3 * d)
            o = sb_attention(qkv.reshape(b, s, 3 * d), n_pairs)
        elif kind == 1:
            qkv = norm_matmul(h, attn_norm[i], w_in_fox, j, 3 * d)
            c, c_t = fox_cumulative_log_forget(h.reshape(b, s, d), attn_norm[i], w_in_fox[j][:, 3 * d:], b_forget[j])
            o = fox_attention(qkv.reshape(b, s, 3 * d), c, c_t, n_pairs)
        else:
            proj = norm_matmul(h, attn_norm[i], w_in_swa, j, 2 * d)
            o = swa_attention(proj.reshape(b, s, 2 * d), sinks[j], rel_bias, n_pairs)
        h = post_attention(o.reshape(t, d), h, p, w_out[kind], mlp_norm[i], w_up, w_down, ple_norm[i], w_ple_gate,
                           w_ple, final_norm, layer=i, out_layer=j, final=(i == depth - 1))
    return h.reshape(b, s, d)
```
